```python
import math
import jax, jax.numpy as jnp
from jax import lax
import numpy as np

D_MODEL = 1024
BATCH = 16
SEQ = 2048
DEPTH = 2

HEAD_DIM = 64
RW_HEADS = D_MODEL // (2 * HEAD_DIM)
RW_DIM = RW_HEADS * HEAD_DIM
DECAY_LORA = 64
AAA_LORA = 64
GATE_LORA = 128
RW_COLS = 3 * RW_DIM + DECAY_LORA + AAA_LORA + GATE_LORA
ATT_HEADS = D_MODEL // (2 * HEAD_DIM)
ATT_DIM = ATT_HEADS * HEAD_DIM
ATT_COLS = 3 * ATT_DIM
AB_COLS = RW_COLS + ATT_COLS
MOBA_BLOCK = 256
MOBA_TOPK = 3
MOBA_Q_BLOCK = 32
NUM_BUCKETS = 32
REL_MAX_DIST = 1024
CONV_WIDTH = 3
D_FF = 2816
NORM_EPS = 1e-6
LNX_EPS = 64e-5
NEG_INF = -1e30

kernel_name = "rwkv7_moba_shortconv_convffn_hybrid"


def rms_norm(x, g):
    xf = x.astype(jnp.float32)
    y = xf * lax.rsqrt(jnp.mean(xf * xf, axis=-1, keepdims=True) + NORM_EPS)
    return (y * g.astype(jnp.float32)).astype(x.dtype)


def causal_dwconv(x, w):
    k_width, chans = w.shape
    return lax.conv_general_dilated(
        x, w[:, None, :].astype(x.dtype), window_strides=(1,), padding=[(k_width - 1, 0)],
        dimension_numbers=("NWC", "WIO", "NWC"), feature_group_count=chans)


def token_shift(p):
    return jnp.pad(p, ((0, 0), (1, 0), (0, 0)))[:, :-1]


def t5_bucket(dist):
    n = jnp.maximum(dist, 0)
    max_exact = NUM_BUCKETS // 2
    nf = jnp.maximum(n, 1).astype(jnp.float32)
    large = max_exact + (jnp.log(nf / max_exact) / math.log(REL_MAX_DIST / max_exact)
                         * (NUM_BUCKETS - max_exact)).astype(jnp.int32)
    return jnp.where(n < max_exact, n, jnp.minimum(large, NUM_BUCKETS - 1))


def rwkv7_time_mix(p, w0, w_lora_up, a0, a_lora_up, g_lora_up, k_k, k_a, r_k, lnx_w, lnx_b):
    B, T, _ = p.shape
    H, N = RW_HEADS, HEAD_DIM
    f32 = jnp.float32
    r, k, v, dw, da, dg = jnp.split(
        p, [RW_DIM, 2 * RW_DIM, 3 * RW_DIM, 3 * RW_DIM + DECAY_LORA,
            3 * RW_DIM + DECAY_LORA + AAA_LORA], axis=-1)
    logw = -jax.nn.softplus(-(w0 + jnp.tanh(dw) @ w_lora_up)) - 0.5
    a = jax.nn.sigmoid(a0 + da @ a_lora_up)
    g = jax.nn.sigmoid(dg) @ g_lora_up
    kk = (k * k_k).reshape(B, T, H, N).astype(f32)
    kk = kk * lax.rsqrt(jnp.maximum(jnp.sum(kk * kk, -1, keepdims=True), 1e-24))
    k = k * (1 + (a - 1) * k_a)
    decay = jnp.exp(-jnp.exp(logw.astype(f32)))

    def heads_tm(t):
        return t.reshape(B, T, H, N).astype(f32).transpose(1, 0, 2, 3)

    xs = (heads_tm(r), heads_tm(decay), heads_tm(k), heads_tm(v),
          kk.transpose(1, 0, 2, 3), heads_tm(a))

    def step(S, inp):
        r_t, w_t, k_t, v_t, kk_t, a_t = inp
        sa = jnp.einsum("bhvk,bhk->bhv", S, -kk_t)
        S = (S * w_t[:, :, None, :] + sa[..., None] * (kk_t * a_t)[:, :, None, :]
             + v_t[..., None] * k_t[:, :, None, :])
        return S, jnp.einsum("bhvk,bhk->bhv", S, r_t)

    _, y = lax.scan(step, jnp.zeros((B, H, N, N), f32), xs)
    y = y.transpose(1, 0, 2, 3)
    mu = jnp.mean(y, -1, keepdims=True)
    var = jnp.mean(jnp.square(y - mu), -1, keepdims=True)
    y = (y - mu) * lax.rsqrt(var + LNX_EPS) * lnx_w.reshape(H, N).astype(f32) \
        + lnx_b.reshape(H, N).astype(f32)
    rh = r.reshape(B, T, H, N).astype(f32)
    kh = k.reshape(B, T, H, N).astype(f32)
    vh = v.reshape(B, T, H, N).astype(f32)
    y = y + jnp.sum(rh * kh * r_k.astype(f32), -1, keepdims=True) * vh
    y = y.reshape(B, T, RW_DIM) * g.astype(f32)
    return y.astype(p.dtype)


def moba_attention(q, k, v, rel_bias, q_norm, k_norm):
    B, T, H, Dh = q.shape
    f32 = jnp.float32
    q = rms_norm(q, q_norm).transpose(0, 2, 1, 3)
    k = rms_norm(k, k_norm).transpose(0, 2, 1, 3)
    v = v.transpose(0, 2, 1, 3)
    nb = -(-T // MOBA_BLOCK)
    pad = nb * MOBA_BLOCK - T
    kp = jnp.pad(k, ((0, 0), (0, 0), (0, pad), (0, 0)))
    vp = jnp.pad(v, ((0, 0), (0, 0), (0, pad), (0, 0)))
    kb = kp.reshape(B, H, nb, MOBA_BLOCK, Dh)
    vb = vp.reshape(B, H, nb, MOBA_BLOCK, Dh)
    kmean = jnp.mean(kb.astype(f32), axis=3)
    gate = jnp.einsum("bhtd,bhnd->bhtn", q.astype(f32), kmean)
    qblk = jnp.arange(T) // MOBA_BLOCK
    past = jnp.arange(nb)[None, :] < qblk[:, None]
    gate = jnp.where(past, gate, NEG_INF)
    n_sel = min(MOBA_TOPK, nb)
    _, sel = lax.top_k(gate, n_sel)

    nq = T // MOBA_Q_BLOCK
    qc_all = q.reshape(B, H, nq, MOBA_Q_BLOCK, Dh).transpose(2, 0, 1, 3, 4)
    sel_all = sel.reshape(B, H, nq, MOBA_Q_BLOCK, n_sel).transpose(2, 0, 1, 3, 4)
    starts = jnp.arange(nq, dtype=jnp.int32) * MOBA_Q_BLOCK
    scale = Dh ** -0.5
    bias_tbl = rel_bias.T.astype(f32)
    head_ix = jnp.arange(H)[:, None, None, None]
    offs = jnp.arange(MOBA_BLOCK)
    gather = jax.vmap(jax.vmap(lambda blocks, ix: blocks[ix]))
    n_keys_sel = n_sel * MOBA_BLOCK

    def one_chunk(args):
        q_c, sel_c, start = args
        pos = start + jnp.arange(MOBA_Q_BLOCK)
        blk = start // MOBA_BLOCK
        k_sel = gather(kb, sel_c)
        v_sel = gather(vb, sel_c)
        s_sel = jnp.einsum("bhqd,bhqjkd->bhqjk", q_c, k_sel).astype(f32) * scale
        kpos = sel_c[..., None] * MOBA_BLOCK + offs
        s_sel = s_sel + bias_tbl[head_ix, t5_bucket(pos[:, None, None] - kpos)]
        s_sel = jnp.where((sel_c < blk)[..., None], s_sel, NEG_INF)
        k_own = lax.dynamic_slice_in_dim(kp, blk * MOBA_BLOCK, MOBA_BLOCK, axis=2)
        v_own = lax.dynamic_slice_in_dim(vp, blk * MOBA_BLOCK, MOBA_BLOCK, axis=2)
        dist_own = pos[:, None] - (blk * MOBA_BLOCK + offs)[None, :]
        s_own = jnp.einsum("bhqd,bhkd->bhqk", q_c, k_own).astype(f32) * scale \
            + bias_tbl[:, t5_bucket(dist_own)]
        s_own = jnp.where(dist_own >= 0, s_own, NEG_INF)
        logits = jnp.concatenate(
            [s_sel.reshape(B, H, MOBA_Q_BLOCK, n_keys_sel), s_own], axis=-1)
        probs = jax.nn.softmax(logits, axis=-1).astype(v.dtype)
        out = jnp.einsum("bhqk,bhqkd->bhqd", probs[..., :n_keys_sel],
                         v_sel.reshape(B, H, MOBA_Q_BLOCK, n_keys_sel, Dh)) \
            + jnp.einsum("bhqk,bhkd->bhqd", probs[..., n_keys_sel:], v_own)
        return out

    out = lax.map(one_chunk, (qc_all, sel_all, starts))
    return out.transpose(1, 0, 3, 2, 4).reshape(B, T, H * Dh)


def rwkv_moba_mixer(x, rel_bias, mix_norm, w_in, shift_mu, w0, w_lora_up, a0, a_lora_up,
                    g_lora_up, k_k, k_a, r_k, lnx_w, lnx_b, q_norm, k_norm, w_out):
    B, T, _ = x.shape
    p = rms_norm(x, mix_norm) @ w_in
    p_rw, p_att = p[..., :RW_COLS], p[..., RW_COLS:]
    p_rw = p_rw + (token_shift(p_rw) - p_rw) * shift_mu
    y_rw = rwkv7_time_mix(p_rw, w0, w_lora_up, a0, a_lora_up, g_lora_up, k_k, k_a, r_k,
                          lnx_w, lnx_b)
    q, k, v = jnp.split(p_att, 3, axis=-1)
    hs = (B, T, ATT_HEADS, HEAD_DIM)
    y_att = moba_attention(q.reshape(hs), k.reshape(hs), v.reshape(hs), rel_bias, q_norm, k_norm)
    return jnp.concatenate([y_rw, y_att.astype(y_rw.dtype)], axis=-1) @ w_out


def short_conv_mixer(x, mix_norm, w_in, conv_w, w_out):
    b_gate, c_gate, h = jnp.split(rms_norm(x, mix_norm) @ w_in, 3, axis=-1)
    return (b_gate * causal_dwconv(c_gate * h, conv_w)) @ w_out


def conv_glu_ffn(x, ffn_norm, up, conv_w, conv_b, down):
    z = causal_dwconv(rms_norm(x, ffn_norm) @ up, conv_w) + conv_b
    a, u = jnp.split(z, 2, axis=-1)
    return (jax.nn.silu(a) * u) @ down


def setup_inputs(seed: int = 0) -> dict:
    key = jax.random.key(seed)
    keys = iter(list(jax.random.split(key, 48)))
    d = D_MODEL

    def nrm(shape, scale):
        return scale * jax.random.normal(next(keys), shape, jnp.float32)

    def gain(n):
        return 1.0 + nrm((n,), 0.05)

    def uni(shape, lo, hi):
        return jax.random.uniform(next(keys), shape, jnp.float32, lo, hi)

    inp = {}
    inp["x"] = nrm((BATCH, SEQ, d), 1.0)
    inp["rel_bias"] = nrm((NUM_BUCKETS, ATT_HEADS), 0.5)
    inp["l0_mix_norm"] = gain(d)
    inp["l0_w_in"] = nrm((d, AB_COLS), d ** -0.5)
    inp["l0_shift_mu"] = uni((RW_COLS,), 0.0, 1.0)
    inp["l0_w0"] = uni((RW_DIM,), -6.0, -1.0)
    inp["l0_w_lora_up"] = nrm((DECAY_LORA, RW_DIM), 0.1)
    inp["l0_a0"] = nrm((RW_DIM,), 0.5)
    inp["l0_a_lora_up"] = nrm((AAA_LORA, RW_DIM), AAA_LORA ** -0.5)
    inp["l0_g_lora_up"] = nrm((GATE_LORA, RW_DIM), GATE_LORA ** -0.5)
    inp["l0_k_k"] = 0.85 + nrm((RW_DIM,), 0.05)
    inp["l0_k_a"] = 1.0 + nrm((RW_DIM,), 0.05)
    inp["l0_r_k"] = nrm((RW_HEADS, HEAD_DIM), 0.1)
    inp["l0_lnx_w"] = gain(RW_DIM)
    inp["l0_lnx_b"] = nrm((RW_DIM,), 0.02)
    inp["l0_q_norm"] = gain(HEAD_DIM)
    inp["l0_k_norm"] = gain(HEAD_DIM)
    inp["l0_w_out"] = nrm((RW_DIM + ATT_DIM, d), (RW_DIM + ATT_DIM) ** -0.5)
    inp["l0_ffn_norm"] = gain(d)
    inp["l0_ffn_up"] = nrm((d, 2 * D_FF), d ** -0.5)
    inp["l0_ffn_conv_w"] = nrm((CONV_WIDTH, 2 * D_FF), CONV_WIDTH ** -0.5)
    inp["l0_ffn_conv_b"] = nrm((2 * D_FF,), 0.02)
    inp["l0_ffn_down"] = nrm((D_FF, d), D_FF ** -0.5)
    inp["l1_mix_norm"] = gain(d)
    inp["l1_w_in"] = nrm((d, 3 * d), d ** -0.5)
    inp["l1_conv_w"] = nrm((CONV_WIDTH, d), CONV_WIDTH ** -0.5)
    inp["l1_w_out"] = nrm((d, d), d ** -0.5)
    inp["l1_ffn_norm"] = gain(d)
    inp["l1_ffn_up"] = nrm((d, 2 * D_FF), d ** -0.5)
    inp["l1_ffn_conv_w"] = nrm((CONV_WIDTH, 2 * D_FF), CONV_WIDTH ** -0.5)
    inp["l1_ffn_conv_b"] = nrm((2 * D_FF,), 0.02)
    inp["l1_ffn_down"] = nrm((D_FF, d), D_FF ** -0.5)
    return inp


def reference(x, rel_bias,
              l0_mix_norm, l0_w_in, l0_shift_mu, l0_w0, l0_w_lora_up, l0_a0, l0_a_lora_up,
              l0_g_lora_up, l0_k_k, l0_k_a, l0_r_k, l0_lnx_w, l0_lnx_b, l0_q_norm, l0_k_norm,
              l0_w_out, l0_ffn_norm, l0_ffn_up, l0_ffn_conv_w, l0_ffn_conv_b, l0_ffn_down,
              l1_mix_norm, l1_w_in, l1_conv_w, l1_w_out, l1_ffn_norm, l1_ffn_up, l1_ffn_conv_w,
              l1_ffn_conv_b, l1_ffn_down):
    mixer_params = (
        (l0_mix_norm, l0_w_in, l0_shift_mu, l0_w0, l0_w_lora_up, l0_a0, l0_a_lora_up,
         l0_g_lora_up, l0_k_k, l0_k_a, l0_r_k, l0_lnx_w, l0_lnx_b, l0_q_norm, l0_k_norm,
         l0_w_out),
        (l1_mix_norm, l1_w_in, l1_conv_w, l1_w_out),
    )
    ffn_params = (
        (l0_ffn_norm, l0_ffn_up, l0_ffn_conv_w, l0_ffn_conv_b, l0_ffn_down),
        (l1_ffn_norm, l1_ffn_up, l1_ffn_conv_w, l1_ffn_conv_b, l1_ffn_down),
    )
    for layer in range(DEPTH):
        if layer % 2 == 0:
            x = x + rwkv_moba_mixer(x, rel_bias, *mixer_params[layer])
        else:
            x = x + short_conv_mixer(x, *mixer_params[layer])
        x = x + conv_glu_ffn(x, *ffn_params[layer])
    return x
```

```python
import functools
import math

import jax
import jax.numpy as jnp
from jax import lax
from jax.experimental import pallas as pl
from jax.experimental.pallas import tpu as pltpu

F32 = jnp.float32
BF16 = jnp.bfloat16

D_MODEL = 1024
HEAD_DIM = 64
RW_DIM = 512
RW_COLS = 3 * RW_DIM + 64 + 64 + 128
ATT_DIM = 512
MOBA_BLOCK = 256
MOBA_TOPK = 3
NUM_BUCKETS = 32
REL_MAX_DIST = 1024
D_FF = 2816
NORM_EPS = 1e-6
LNX_EPS = 64e-5
NEG_INF = -1e30

CHUNK = 64
LANES = 128
VMEM_LIMIT = 52 * 1024 * 1024


def _dot(a, b):
    return jnp.dot(a, b, preferred_element_type=F32)


def _dot_nt(a, b):
    return lax.dot_general(a, b, (((1,), (1,)), ((), ())), preferred_element_type=F32)


def _dot_split(x, w):
    hi = x.astype(BF16)
    lo = (x - hi.astype(F32)).astype(BF16)
    return _dot(hi, w) + _dot(lo, w)


def _rms_norm_rows(x, g):
    return x * lax.rsqrt(jnp.mean(x * x, axis=-1, keepdims=True) + NORM_EPS) * g


def _const_spec(shape):
    return pl.BlockSpec(shape, lambda *_: (0,) * len(shape), pipeline_mode=pl.Buffered(1))


def _params(*sem):
    return pltpu.CompilerParams(dimension_semantics=sem, vmem_limit_bytes=VMEM_LIMIT)


def _inproj0_body(x_ref, g_ref, wrw_ref, wk_ref, wqt_ref, wvt_ref, prw_ref, k_ref, qt_ref, vt_ref):
    xn = _rms_norm_rows(x_ref[...], g_ref[...]).astype(BF16)
    prw_ref[...] = _dot(xn, wrw_ref[...])
    k_ref[...] = _dot(xn, wk_ref[...]).astype(BF16)
    qt_ref[...] = _dot_nt(wqt_ref[...], xn).astype(BF16)
    vt_ref[...] = _dot_nt(wvt_ref[...], xn).astype(BF16)


def _inproj0(x2, g, w_rw, w_k, w_qt, w_vt, tm=512):
    n = x2.shape[0]
    return pl.pallas_call(
        _inproj0_body,
        grid=(n // tm,),
        in_specs=[
            pl.BlockSpec((tm, D_MODEL), lambda i: (i, 0)),
            _const_spec((1, D_MODEL)),
            _const_spec((D_MODEL, RW_COLS)),
            _const_spec((D_MODEL, ATT_DIM)),
            _const_spec((ATT_DIM, D_MODEL)),
            _const_spec((ATT_DIM, D_MODEL)),
        ],
        out_specs=[
            pl.BlockSpec((tm, RW_COLS), lambda i: (i, 0)),
            pl.BlockSpec((tm, ATT_DIM), lambda i: (i, 0)),
            pl.BlockSpec((ATT_DIM, tm), lambda i: (0, i)),
            pl.BlockSpec((ATT_DIM, tm), lambda i: (0, i)),
        ],
        out_shape=[
            jax.ShapeDtypeStruct((n, RW_COLS), F32),
            jax.ShapeDtypeStruct((n, ATT_DIM), BF16),
            jax.ShapeDtypeStruct((ATT_DIM, n), BF16),
            jax.ShapeDtypeStruct((ATT_DIM, n), BF16),
        ],
        compiler_params=_params("arbitrary"),
        name="inproj0",
    )(x2, g, w_rw, w_k, w_qt, w_vt)


def _rwkv_body(p_ref, mu_ref, w0_ref, a0_ref, wwa_ref, gup_ref, kk_ref, ka_ref, rk_ref, lnw_ref, lnb_ref,
               tri_ref, ones_ref, y_ref, carry_ref, state_ref):
    L = CHUNK

    @pl.when(pl.program_id(1) == 0)
    def _():
        carry_ref[...] = jnp.zeros_like(carry_ref)
        state_ref[...] = jnp.zeros_like(state_ref)

    p = p_ref[...]
    row = lax.broadcasted_iota(jnp.int32, p.shape, 0)
    prev = jnp.where(row == 0, carry_ref[...], pltpu.roll(p, 1, axis=0))
    carry_ref[...] = p[L - 1:L, :]
    ps = p + (prev - p) * mu_ref[...]
    r = ps[:, 0:RW_DIM]
    k = ps[:, RW_DIM:2 * RW_DIM]
    v = ps[:, 2 * RW_DIM:3 * RW_DIM]
    slab = ps[:, 3 * RW_DIM:3 * RW_DIM + LANES]
    dg = ps[:, 3 * RW_DIM + LANES:RW_COLS]
    lo = lax.broadcasted_iota(jnp.int32, (L, LANES), 1) < HEAD_DIM

    wa = _dot(jnp.where(lo, jnp.tanh(slab), slab).astype(BF16), wwa_ref[...])
    ld = -math.exp(-0.5) * jax.nn.sigmoid(w0_ref[...] + wa[:, :RW_DIM])
    a = jax.nn.sigmoid(a0_ref[...] + wa[:, RW_DIM:])
    g = _dot(jax.nn.sigmoid(dg).astype(BF16), gup_ref[...])
    ones_bd = ones_ref[...]
    kk = k * kk_ref[...]
    kk = kk * lax.rsqrt(jnp.maximum(_dot_split(kk * kk, ones_bd), 1e-24))
    kmod = k * (1.0 + (a - 1.0) * ka_ref[...])

    tri = tri_ref[...]
    ld_hi = ld.astype(BF16)
    rem = ld - ld_hi.astype(F32)
    ld_mid = rem.astype(BF16)
    ld_lo = (rem - ld_mid.astype(F32)).astype(BF16)
    cs = _dot(tri, ld_hi) + _dot(tri, ld_mid) + _dot(tri, ld_lo)
    e_c = jnp.exp(cs)
    e_nc = jnp.exp(-cs)
    abar = -kk * jnp.exp(cs - ld)
    rbar = r * e_c
    bbar = kk * a * e_nc
    kbar = kmod * e_nc
    wl = e_c[L - 1:L, :]
    bw = bbar * wl
    kw = kbar * wl

    ri = lax.broadcasted_iota(jnp.int32, (LANES, LANES), 0)
    ci = lax.broadcasted_iota(jnp.int32, (LANES, LANES), 1)
    same = (ri >= HEAD_DIM) == (ci >= HEAD_DIM)
    t_i = ri & (L - 1)
    s_i = ci & (L - 1)
    m_strict = same & (s_i < t_i)
    m_incl = same & (s_i <= t_i)
    eye = ri == ci

    def stack(z):
        return jnp.concatenate([jnp.where(lo, z, 0.0), jnp.where(lo, 0.0, z)], axis=0)

    ys = []
    for gp in range(RW_DIM // LANES):
        sl = slice(LANES * gp, LANES * (gp + 1))
        ab, rb, bb, kb, vv = abar[:, sl], rbar[:, sl], bbar[:, sl], kbar[:, sl], v[:, sl]
        lhs = jnp.concatenate([stack(ab), stack(rb)], axis=0).astype(BF16)
        rhs = jnp.concatenate([bb, bb, kb, kb], axis=0).astype(BF16)
        q4 = _dot_nt(lhs, rhs)
        a_ab = jnp.where(m_strict, q4[0:LANES, 0:LANES], 0.0)
        a_ak = jnp.where(m_strict, q4[0:LANES, LANES:], 0.0)
        r_b = jnp.where(m_incl, q4[LANES:, 0:LANES], 0.0)
        r_k = jnp.where(m_incl, q4[LANES:, LANES:], 0.0)
        tinv = jnp.where(eye, 1.0, a_ab)
        pw = a_ab
        for _ in range(5):
            pb = pw.astype(BF16)
            pw = _dot(pb, pb)
            tinv = tinv + _dot(tinv.astype(BF16), pw.astype(BF16))
        s0 = state_ref[gp]
        ar = _dot_nt(jnp.concatenate([ab, rb], axis=0).astype(BF16), s0.astype(BF16))
        v_st = stack(vv).astype(BF16)
        rhs_st = stack(ar[0:L]) + _dot(a_ak.astype(BF16), v_st)
        u_st = _dot(tinv.astype(BF16), rhs_st.astype(BF16))
        y_st = _dot(jnp.concatenate([r_b, r_k], axis=1).astype(BF16),
                    jnp.concatenate([u_st.astype(BF16), v_st], axis=0))
        ys.append(ar[L:] + y_st[0:L] + y_st[L:])
        uv = jnp.concatenate([u_st[0:L] + u_st[L:], vv], axis=0)
        bk = jnp.concatenate([bw[:, sl], kw[:, sl]], axis=0)
        s_add = _dot(uv.T.astype(BF16), bk.astype(BF16))
        state_ref[gp] = s0 * wl[:, sl] + jnp.where(same, s_add, 0.0)
    y = jnp.concatenate(ys, axis=1)

    inv_n = 1.0 / HEAD_DIM
    d = y - _dot_split(y, ones_bd) * inv_n
    var = _dot_split(d * d, ones_bd) * inv_n
    yn = d * lax.rsqrt(var + LNX_EPS) * lnw_ref[...] + lnb_ref[...]
    bonus = _dot_split(r * kmod * rk_ref[...], ones_bd) * v
    y_ref[...] = ((yn + bonus) * g).astype(BF16)


def _rwkv(p_rw, batch, seq, mu, w0, a0, w_wa, g_up, k_k, k_a, r_k, lnx_w, lnx_b, tri, ones_bd):
    n = batch * seq
    nc = seq // CHUNK
    vec = lambda: _const_spec((1, RW_DIM))
    return pl.pallas_call(
        _rwkv_body,
        grid=(batch, nc),
        in_specs=[
            pl.BlockSpec((CHUNK, RW_COLS), lambda b, c: (b * nc + c, 0)),
            _const_spec((1, RW_COLS)), vec(), vec(),
            _const_spec((LANES, 2 * RW_DIM)), _const_spec((LANES, RW_DIM)),
            vec(), vec(), vec(), vec(), vec(),
            _const_spec((CHUNK, CHUNK)), _const_spec((RW_DIM, RW_DIM)),
        ],
        out_specs=pl.BlockSpec((CHUNK, RW_DIM), lambda b, c: (b * nc + c, 0)),
        out_shape=jax.ShapeDtypeStruct((n, RW_DIM), BF16),
        scratch_shapes=[
            pltpu.VMEM((1, RW_COLS), F32),
            pltpu.VMEM((RW_DIM // LANES, LANES, LANES), F32),
        ],
        compiler_params=_params("arbitrary", "arbitrary"),
        name="rwkv",
    )(p_rw, mu, w0, a0, w_wa, g_up, k_k, k_a, r_k, lnx_w, lnx_b, tri, ones_bd)


def _bucket_ranges():
    max_exact = NUM_BUCKETS // 2
    ratio_num, ratio_den = REL_MAX_DIST, max_exact
    bounds = []
    for b in range(NUM_BUCKETS):
        if b < max_exact:
            bounds.append((b, b + 1))
            continue
        def first(j):
            n = max_exact
            while (n ** (NUM_BUCKETS - max_exact)) * (ratio_den ** j) < (ratio_num ** j) * (max_exact ** (NUM_BUCKETS - max_exact)):
                n += 1
            return n
        lo = first(b - max_exact)
        hi = first(b + 1 - max_exact) if b < NUM_BUCKETS - 1 else 1 << 30
        bounds.append((lo, hi))
    return bounds


def _t5bias_body(tab_ref, o_ref, *, ranges):
    h = pl.program_id(0)
    blk = MOBA_BLOCK
    rel = lax.broadcasted_iota(jnp.int32, (blk, blk), 1) - lax.broadcasted_iota(jnp.int32, (blk, blk), 0)
    for di in range(o_ref.shape[1]):
        dist = rel + di * blk
        n = jnp.maximum(dist, 0)
        acc = jnp.zeros((blk, blk), F32)
        for b, (lo, hi) in enumerate(ranges):
            if hi <= di * blk - (blk - 1) or lo > di * blk + (blk - 1):
                continue
            acc = jnp.where((n >= lo) & (n < hi), tab_ref[h, b], acc)
        o_ref[0, di] = jnp.where(dist >= 0, acc, NEG_INF)


def _t5bias(table, nblk):
    heads = table.shape[0]
    return pl.pallas_call(
        functools.partial(_t5bias_body, ranges=_bucket_ranges()),
        grid=(heads,),
        in_specs=[pl.BlockSpec(memory_space=pltpu.SMEM)],
        out_specs=pl.BlockSpec((1, nblk, MOBA_BLOCK, MOBA_BLOCK), lambda h: (h, 0, 0, 0)),
        out_shape=jax.ShapeDtypeStruct((heads, nblk, MOBA_BLOCK, MOBA_BLOCK), F32),
        compiler_params=_params("arbitrary"),
        name="t5bias",
    )(table)


def _moba_body(k_ref, qt_ref, vt_ref, bias_ref, qg_ref, kg_ref, ones_ref, o_ref,
               kn_ref, qn_ref, selb_ref, ot_ref):
    seq = k_ref.shape[0]
    blk = MOBA_BLOCK
    nb = seq // blk

    k2 = k_ref[...].astype(F32)
    ms = _dot_split(k2 * k2, ones_ref[...]) * (1.0 / HEAD_DIM)
    knf = k2 * lax.rsqrt(ms + NORM_EPS) * kg_ref[...]
    kn_ref[...] = knf.astype(BF16)
    kmean = jnp.concatenate(
        [jnp.mean(knf[j * blk:(j + 1) * blk], axis=0, keepdims=True) for j in range(nb)], axis=0)

    q2 = qt_ref[...].astype(F32)

    def nrm(qh):
        return qh * lax.rsqrt(jnp.mean(qh * qh, axis=0, keepdims=True) + NORM_EPS)

    qn = jnp.concatenate([nrm(q2[0:HEAD_DIM]), nrm(q2[HEAD_DIM:])], axis=0) * qg_ref[...] * (HEAD_DIM ** -0.5)
    qn_ref[...] = qn.astype(BF16)
    qnb = qn_ref[...]

    km_hi = kmean.astype(BF16)
    km_lo = (kmean - km_hi.astype(F32)).astype(BF16)
    lane = lax.broadcasted_iota(jnp.int32, (nb, LANES), 1)
    rowi = lax.broadcasted_iota(jnp.int32, (nb, seq), 0)
    qblk = lax.broadcasted_iota(jnp.int32, (nb, seq), 1) // blk
    past = rowi < qblk
    qblk1 = qblk[0:1]
    zero_b = jnp.zeros_like(km_hi)

    for hh in range(2):
        hm = (lane < HEAD_DIM) if hh == 0 else (lane >= HEAD_DIM)
        gate = _dot(jnp.where(hm, km_hi, zero_b), qnb) + _dot(jnp.where(hm, km_lo, zero_b), qnb)
        for j in range(nb):
            gj = gate[j:j + 1, :]
            beats = (gate > gj) | ((gate == gj) & (rowi < j))
            cnt = jnp.sum(jnp.where(past & beats, 1.0, 0.0), axis=0, keepdims=True)
            sel = ((qblk1 > j) & (cnt < MOBA_TOPK)) | (qblk1 == j)
            selb_ref[hh, j] = jnp.broadcast_to(jnp.where(sel, 0.0, NEG_INF), (8, seq))

        rmask = lax.broadcasted_iota(jnp.int32, (LANES, blk), 0) < HEAD_DIM
        if hh == 1:
            rmask = jnp.logical_not(rmask)

        def q_tile(i, carry, hh=hh, rmask=rmask):
            qs = pl.multiple_of(i * blk, blk)
            qt = qn_ref[:, pl.ds(qs, blk)]
            qh = jnp.where(rmask, qt, jnp.zeros_like(qt))

            def kv_blk(j, st):
                m, l, acc = st
                ks = pl.multiple_of(j * blk, blk)
                s = _dot(kn_ref[pl.ds(ks, blk), :], qh)
                s = s + bias_ref[hh, i - j] + selb_ref[hh, j, 0:1, pl.ds(qs, blk)]
                m_new = jnp.maximum(m, jnp.max(s, axis=0, keepdims=True))
                p = jnp.exp(s - m_new)
                corr = jnp.exp(m - m_new)
                l = l * corr + jnp.sum(p, axis=0, keepdims=True)
                vt = vt_ref[hh * HEAD_DIM:(hh + 1) * HEAD_DIM, pl.ds(ks, blk)]
                acc = acc * corr + _dot(vt, p.astype(BF16))
                return m_new, l, acc

            init = (jnp.full((1, blk), NEG_INF, F32), jnp.zeros((1, blk), F32), jnp.zeros((HEAD_DIM, blk), F32))
            _, l, acc = lax.fori_loop(0, i + 1, kv_blk, init)
            ot_ref[hh * HEAD_DIM:(hh + 1) * HEAD_DIM, pl.ds(qs, blk)] = acc / l
            return carry

        lax.fori_loop(0, nb, q_tile, 0)

    for i in range(nb):
        o_ref[i * blk:(i + 1) * blk, :] = ot_ref[:, i * blk:(i + 1) * blk].T.astype(BF16)


def _moba(k_att, q_t, v_t, bias_t, qg, kg, ones_bd, batch, seq):
    n = batch * seq
    npair = ATT_DIM // LANES
    nb = seq // MOBA_BLOCK
    return pl.pallas_call(
        _moba_body,
        grid=(npair, batch),
        in_specs=[
            pl.BlockSpec((seq, LANES), lambda g, b: (b, g)),
            pl.BlockSpec((LANES, seq), lambda g, b: (g, b)),
            pl.BlockSpec((LANES, seq), lambda g, b: (g, b)),
            pl.BlockSpec((2, nb, MOBA_BLOCK, MOBA_BLOCK), lambda g, b: (g, 0, 0, 0)),
            _const_spec((LANES, 1)), _const_spec((1, LANES)), _const_spec((LANES, LANES)),
        ],
        out_specs=pl.BlockSpec((seq, LANES), lambda g, b: (b, g)),
        out_shape=jax.ShapeDtypeStruct((n, ATT_DIM), BF16),
        scratch_shapes=[
            pltpu.VMEM((seq, LANES), BF16),
            pltpu.VMEM((LANES, seq), BF16),
            pltpu.VMEM((2, nb, 8, seq), F32),
            pltpu.VMEM((LANES, seq), F32),
        ],
        compiler_params=_params("arbitrary", "arbitrary"),
        name="moba",
    )(k_att, q_t, v_t, bias_t, qg, kg, ones_bd)


def _outproj_body(x_ref, yr_ref, ya_ref, wr_ref, wa_ref, o_ref):
    o_ref[...] = x_ref[...] + _dot(yr_ref[...], wr_ref[...]) + _dot(ya_ref[...], wa_ref[...])


def _outproj(x2, y_rw, y_att, w_r, w_a, tm=1024):
    n = x2.shape[0]
    return pl.pallas_call(
        _outproj_body,
        grid=(n // tm,),
        in_specs=[
            pl.BlockSpec((tm, D_MODEL), lambda i: (i, 0)),
            pl.BlockSpec((tm, RW_DIM), lambda i: (i, 0)),
            pl.BlockSpec((tm, ATT_DIM), lambda i: (i, 0)),
            _const_spec((RW_DIM, D_MODEL)), _const_spec((ATT_DIM, D_MODEL)),
        ],
        out_specs=pl.BlockSpec((tm, D_MODEL), lambda i: (i, 0)),
        out_shape=jax.ShapeDtypeStruct((n, D_MODEL), F32),
        compiler_params=_params("arbitrary"),
        name="outproj",
    )(x2, y_rw, y_att, w_r, w_a)


def _causal_conv3(h, carry_ref, cols, w):
    tm = h.shape[0]
    row = lax.broadcasted_iota(jnp.int32, h.shape, 0)
    c = carry_ref[:, cols]
    h1 = jnp.where(row == 0, c[7:8], pltpu.roll(h, 1, axis=0))
    h2 = jnp.where(row == 0, c[6:7], jnp.where(row == 1, c[7:8], pltpu.roll(h, 2, axis=0)))
    carry_ref[:, cols] = h[tm - 8:tm]
    return w[2:3] * h + w[1:2] * h1 + w[0:1] * h2


def _ffn_body(x_ref, g_ref, up_ref, cw_ref, cb_ref, down_ref, o_ref, xn_ref, carry_ref, *, tc):
    @pl.when(pl.program_id(1) == 0)
    def _():
        carry_ref[...] = jnp.zeros_like(carry_ref)

    x = x_ref[...]
    xn_ref[...] = _rms_norm_rows(x, g_ref[...]).astype(BF16)
    o_ref[...] = x

    def branch(c0):
        cols = pl.ds(c0, tc)
        h = _dot(xn_ref[...], up_ref[:, cols])
        return _causal_conv3(h, carry_ref, cols, cw_ref[:, cols]) + cb_ref[:, cols]

    def chunk(ci, carry):
        c0 = pl.multiple_of(ci * tc, tc)
        za = branch(c0)
        zu = branch(pl.multiple_of(D_FF + c0, LANES))
        z = (za * jax.nn.sigmoid(za) * zu).astype(BF16)
        o_ref[...] += _dot(z, down_ref[pl.ds(c0, tc), :])
        return carry

    lax.fori_loop(0, D_FF // tc, chunk, 0)


def _ffn(x2, batch, seq, g, up, conv_w, conv_b, down, tm=512, tc=256):
    n = batch * seq
    nt = seq // tm
    return pl.pallas_call(
        functools.partial(_ffn_body, tc=tc),
        grid=(batch, nt),
        in_specs=[
            pl.BlockSpec((tm, D_MODEL), lambda b, t: (b * nt + t, 0)),
            _const_spec((1, D_MODEL)),
            _const_spec((D_MODEL, 2 * D_FF)),
            _const_spec((3, 2 * D_FF)),
            _const_spec((1, 2 * D_FF)),
            _const_spec((D_FF, D_MODEL)),
        ],
        out_specs=pl.BlockSpec((tm, D_MODEL), lambda b, t: (b * nt + t, 0)),
        out_shape=jax.ShapeDtypeStruct((n, D_MODEL), F32),
        scratch_shapes=[pltpu.VMEM((tm, D_MODEL), BF16), pltpu.VMEM((8, 2 * D_FF), F32)],
        compiler_params=_params("arbitrary", "arbitrary"),
        name="ffn",
    )(x2, g, up, conv_w, conv_b, down)


def _sconv_body(x_ref, g_ref, win_ref, cw_ref, wout_ref, o_ref, xn_ref, carry_ref, *, tc):
    @pl.when(pl.program_id(1) == 0)
    def _():
        carry_ref[...] = jnp.zeros_like(carry_ref)

    x = x_ref[...]
    xn_ref[...] = _rms_norm_rows(x, g_ref[...]).astype(BF16)
    o_ref[...] = x

    def chunk(ci, carry):
        c0 = pl.multiple_of(ci * tc, tc)
        cols = pl.ds(c0, tc)
        xn = xn_ref[...]
        b_gate = _dot(xn, win_ref[:, cols])
        c_gate = _dot(xn, win_ref[:, pl.ds(pl.multiple_of(D_MODEL + c0, tc), tc)])
        hid = _dot(xn, win_ref[:, pl.ds(pl.multiple_of(2 * D_MODEL + c0, tc), tc)])
        conv = _causal_conv3(c_gate * hid, carry_ref, cols, cw_ref[:, cols])
        o_ref[...] += _dot((b_gate * conv).astype(BF16), wout_ref[cols, :])
        return carry

    lax.fori_loop(0, D_MODEL // tc, chunk, 0)


def _sconv(x2, batch, seq, g, w_in, conv_w, w_out, tm=512, tc=256):
    n = batch * seq
    nt = seq // tm
    return pl.pallas_call(
        functools.partial(_sconv_body, tc=tc),
        grid=(batch, nt),
        in_specs=[
            pl.BlockSpec((tm, D_MODEL), lambda b, t: (b * nt + t, 0)),
            _const_spec((1, D_MODEL)),
            _const_spec((D_MODEL, 3 * D_MODEL)),
            _const_spec((3, D_MODEL)),
            _const_spec((D_MODEL, D_MODEL)),
        ],
        out_specs=pl.BlockSpec((tm, D_MODEL), lambda b, t: (b * nt + t, 0)),
        out_shape=jax.ShapeDtypeStruct((n, D_MODEL), F32),
        scratch_shapes=[pltpu.VMEM((tm, D_MODEL), BF16), pltpu.VMEM((8, D_MODEL), F32)],
        compiler_params=_params("arbitrary", "arbitrary"),
        name="sconv",
    )(x2, g, w_in, conv_w, w_out)


def _block_diag_ones(n):
    i = jnp.arange(n) // HEAD_DIM
    return (i[:, None] == i[None, :]).astype(BF16)


def _rwkv_moba_mixer(x2, batch, seq, rel_bias, mix_norm, w_in, shift_mu, w0, w_lora_up, a0, a_lora_up, g_lora_up,
                     k_k, k_a, r_k, lnx_w, lnx_b, q_norm, k_norm, w_out):
    row = lambda z: z.reshape(1, -1).astype(F32)
    w_in = w_in.astype(BF16)
    q0 = RW_COLS
    w_rw = w_in[:, :q0]
    w_qt = w_in[:, q0:q0 + ATT_DIM].T
    w_k = w_in[:, q0 + ATT_DIM:q0 + 2 * ATT_DIM]
    w_vt = w_in[:, q0 + 2 * ATT_DIM:].T
    p_rw, k_att, q_t, v_t = _inproj0(x2, row(mix_norm), w_rw, w_k, w_qt, w_vt)

    zeros = jnp.zeros_like(w_lora_up)
    w_wa = jnp.concatenate([jnp.concatenate([w_lora_up, zeros], axis=1),
                            jnp.concatenate([zeros, a_lora_up], axis=1)], axis=0).astype(BF16)
    tri = (jnp.arange(CHUNK)[:, None] >= jnp.arange(CHUNK)[None, :]).astype(BF16)
    y_rw = _rwkv(p_rw, batch, seq, row(shift_mu), row(w0), row(a0), w_wa, g_lora_up.astype(BF16),
                 row(k_k), row(k_a), row(r_k), row(lnx_w), row(lnx_b), tri, _block_diag_ones(RW_DIM))

    bias_t = _t5bias(rel_bias.T.astype(F32), seq // MOBA_BLOCK)
    qg = jnp.tile(q_norm.astype(F32), 2).reshape(LANES, 1)
    kg = jnp.tile(k_norm.astype(F32), 2).reshape(1, LANES)
    y_att = _moba(k_att, q_t, v_t, bias_t, qg, kg, _block_diag_ones(LANES), batch, seq)

    w_out = w_out.astype(BF16)
    return _outproj(x2, y_rw, y_att, w_out[:RW_DIM], w_out[RW_DIM:])


def kernel(x, rel_bias, l0_mix_norm, l0_w_in, l0_shift_mu, l0_w0, l0_w_lora_up, l0_a0, l0_a_lora_up, l0_g_lora_up, l0_k_k, l0_k_a, l0_r_k, l0_lnx_w, l0_lnx_b, l0_q_norm, l0_k_norm, l0_w_out, l0_ffn_norm, l0_ffn_up, l0_ffn_conv_w, l0_ffn_conv_b, l0_ffn_down, l1_mix_norm, l1_w_in, l1_conv_w, l1_w_out, l1_ffn_norm, l1_ffn_up, l1_ffn_conv_w, l1_ffn_conv_b, l1_ffn_down):
    batch, seq, d = x.shape
    row = lambda z: z.reshape(1, -1).astype(F32)
    x2 = x.reshape(batch * seq, d)
    x2 = _rwkv_moba_mixer(x2, batch, seq, rel_bias, l0_mix_norm, l0_w_in, l0_shift_mu, l0_w0, l0_w_lora_up, l0_a0,
                          l0_a_lora_up, l0_g_lora_up, l0_k_k, l0_k_a, l0_r_k, l0_lnx_w, l0_lnx_b, l0_q_norm,
                          l0_k_norm, l0_w_out)
    x2 = _ffn(x2, batch, seq, row(l0_ffn_norm), l0_ffn_up.astype(BF16), l0_ffn_conv_w.astype(F32),
              row(l0_ffn_conv_b), l0_ffn_down.astype(BF16))
    x2 = _sconv(x2, batch, seq, row(l1_mix_norm), l1_w_in.astype(BF16), l1_conv_w.astype(F32),
                l1_w_out.astype(BF16))
    x2 = _ffn(x2, batch, seq, row(l1_ffn_norm), l1_ffn_up.astype(BF16), l1_ffn_conv_w.astype(F32),
              row(l1_ffn_conv_b), l1_ffn_down.astype(BF16))
    return x2.reshape(batch, seq, d)
```

```python
import functools
import math

import jax
import jax.numpy as jnp
from jax import lax
from jax.experimental import pallas as pl
from jax.experimental.pallas import tpu as pltpu

F32 = jnp.float32
BF16 = jnp.bfloat16

D_MODEL = 1024
HEAD_DIM = 64
RW_DIM = 512
RW_COLS = 3 * RW_DIM + 64 + 64 + 128
ATT_DIM = 512
MOBA_BLOCK = 256
MOBA_TOPK = 3
NUM_BUCKETS = 32
REL_MAX_DIST = 1024
D_FF = 2816
NORM_EPS = 1e-6
LNX_EPS = 64e-5
NEG_INF = -1e30

CHUNK = 64
LANES = 128
VMEM_LIMIT = 52 * 1024 * 1024


def _dot(a, b):
    return jnp.dot(a, b, preferred_element_type=F32)


def _dot_nt(a, b):
    return lax.dot_general(a, b, (((1,), (1,)), ((), ())), preferred_element_type=F32)


def _dot_split(x, w):
    hi = x.astype(BF16)
    lo = (x - hi.astype(F32)).astype(BF16)
    return _dot(hi, w) + _dot(lo, w)


def _rms_norm_rows(x, g):
    return x * lax.rsqrt(jnp.mean(x * x, axis=-1, keepdims=True) + NORM_EPS) * g


def _const_spec(shape):
    return pl.BlockSpec(shape, lambda *_: (0,) * len(shape), pipeline_mode=pl.Buffered(1))


def _params(*sem):
    return pltpu.CompilerParams(dimension_semantics=sem, vmem_limit_bytes=VMEM_LIMIT)


def _inproj0_body(x_ref, g_ref, wrw_ref, wk_ref, wqt_ref, wvt_ref, prw_ref, k_ref, qt_ref, vt_ref):
    xn = _rms_norm_rows(x_ref[...], g_ref[...]).astype(BF16)
    prw_ref[...] = _dot(xn, wrw_ref[...])
    k_ref[...] = _dot(xn, wk_ref[...]).astype(BF16)
    qt_ref[...] = _dot_nt(wqt_ref[...], xn).astype(BF16)
    vt_ref[...] = _dot_nt(wvt_ref[...], xn).astype(BF16)


def _inproj0(x2, g, w_rw, w_k, w_qt, w_vt, tm=512):
    n = x2.shape[0]
    return pl.pallas_call(
        _inproj0_body,
        grid=(n // tm,),
        in_specs=[
            pl.BlockSpec((tm, D_MODEL), lambda i: (i, 0)),
            _const_spec((1, D_MODEL)),
            _const_spec((D_MODEL, RW_COLS)),
            _const_spec((D_MODEL, ATT_DIM)),
            _const_spec((ATT_DIM, D_MODEL)),
            _const_spec((ATT_DIM, D_MODEL)),
        ],
        out_specs=[
            pl.BlockSpec((tm, RW_COLS), lambda i: (i, 0)),
            pl.BlockSpec((tm, ATT_DIM), lambda i: (i, 0)),
            pl.BlockSpec((ATT_DIM, tm), lambda i: (0, i)),
            pl.BlockSpec((ATT_DIM, tm), lambda i: (0, i)),
        ],
        out_shape=[
            jax.ShapeDtypeStruct((n, RW_COLS), F32),
            jax.ShapeDtypeStruct((n, ATT_DIM), BF16),
            jax.ShapeDtypeStruct((ATT_DIM, n), BF16),
            jax.ShapeDtypeStruct((ATT_DIM, n), BF16),
        ],
        compiler_params=_params("arbitrary"),
        name="inproj0",
    )(x2, g, w_rw, w_k, w_qt, w_vt)


def _rwkv_body(p_ref, mu_ref, w0_ref, a0_ref, wwa_ref, gup_ref, kk_ref, ka_ref, rk_ref, lnw_ref, lnb_ref,
               tri_ref, ones_ref, y_ref, carry_ref, state_ref):
    L = CHUNK

    @pl.when(pl.program_id(1) == 0)
    def _():
        carry_ref[...] = jnp.zeros_like(carry_ref)
        state_ref[...] = jnp.zeros_like(state_ref)

    p = p_ref[...]
    row = lax.broadcasted_iota(jnp.int32, p.shape, 0)
    prev = jnp.where(row == 0, carry_ref[...], pltpu.roll(p, 1, axis=0))
    carry_ref[...] = p[L - 1:L, :]
    ps = p + (prev - p) * mu_ref[...]
    r = ps[:, 0:RW_DIM]
    k = ps[:, RW_DIM:2 * RW_DIM]
    v = ps[:, 2 * RW_DIM:3 * RW_DIM]
    slab = ps[:, 3 * RW_DIM:3 * RW_DIM + LANES]
    dg = ps[:, 3 * RW_DIM + LANES:RW_COLS]
    lo = lax.broadcasted_iota(jnp.int32, (L, LANES), 1) < HEAD_DIM

    wa = _dot(jnp.where(lo, jnp.tanh(slab), slab).astype(BF16), wwa_ref[...])
    ld = -math.exp(-0.5) * jax.nn.sigmoid(w0_ref[...] + wa[:, :RW_DIM])
    a = jax.nn.sigmoid(a0_ref[...] + wa[:, RW_DIM:])
    g = _dot(jax.nn.sigmoid(dg).astype(BF16), gup_ref[...])
    ones_bd = ones_ref[...]
    kk = k * kk_ref[...]
    kk = kk * lax.rsqrt(jnp.maximum(_dot_split(kk * kk, ones_bd), 1e-24))
    kmod = k * (1.0 + (a - 1.0) * ka_ref[...])

    tri = tri_ref[...]
    ld_hi = ld.astype(BF16)
    rem = ld - ld_hi.astype(F32)
    ld_mid = rem.astype(BF16)
    ld_lo = (rem - ld_mid.astype(F32)).astype(BF16)
    cs = _dot(tri, ld_hi) + _dot(tri, ld_mid) + _dot(tri, ld_lo)
    e_c = jnp.exp(cs)
    e_nc = jnp.exp(-cs)
    abar = -kk * jnp.exp(cs - ld)
    rbar = r * e_c
    bbar = kk * a * e_nc
    kbar = kmod * e_nc
    wl = e_c[L - 1:L, :]
    bw = bbar * wl
    kw = kbar * wl

    ri = lax.broadcasted_iota(jnp.int32, (LANES, LANES), 0)
    ci = lax.broadcasted_iota(jnp.int32, (LANES, LANES), 1)
    same = (ri >= HEAD_DIM) == (ci >= HEAD_DIM)
    t_i = ri & (L - 1)
    s_i = ci & (L - 1)
    m_strict = same & (s_i < t_i)
    m_incl = same & (s_i <= t_i)
    eye = ri == ci

    def stack(z):
        return jnp.concatenate([jnp.where(lo, z, 0.0), jnp.where(lo, 0.0, z)], axis=0)

    pairs = range(RW_DIM // LANES)
    sls = [slice(LANES * gp, LANES * (gp + 1)) for gp in pairs]
    ab = [abar[:, sl] for sl in sls]
    rb = [rbar[:, sl] for sl in sls]
    vv = [v[:, sl] for sl in sls]
    s0 = [state_ref[gp] for gp in pairs]
    q4 = [_dot_nt(jnp.concatenate([stack(ab[gp]), stack(rb[gp])], axis=0).astype(BF16),
                  jnp.concatenate([bbar[:, sls[gp]]] * 2 + [kbar[:, sls[gp]]] * 2, axis=0).astype(BF16))
          for gp in pairs]
    ar = [_dot_nt(jnp.concatenate([ab[gp], rb[gp]], axis=0).astype(BF16), s0[gp].astype(BF16)) for gp in pairs]
    a_ab = [jnp.where(m_strict, q[0:LANES, 0:LANES], 0.0) for q in q4]
    tinv = [jnp.where(eye, 1.0, a) for a in a_ab]
    pw = []
    for a in a_ab:
        a16 = a.astype(BF16)
        pw.append(_dot(a16, a16).astype(BF16))
    for _ in range(4):
        both = [_dot(jnp.concatenate([tinv[gp].astype(BF16), pw[gp]], axis=0), pw[gp]) for gp in pairs]
        tinv = [tinv[gp] + both[gp][0:LANES] for gp in pairs]
        pw = [both[gp][LANES:].astype(BF16) for gp in pairs]
    tinv = [tinv[gp] + _dot(tinv[gp].astype(BF16), pw[gp]) for gp in pairs]
    v_st = [stack(z).astype(BF16) for z in vv]
    rhs_st = [stack(ar[gp][0:L]) + _dot(jnp.where(m_strict, q4[gp][0:LANES, LANES:], 0.0).astype(BF16), v_st[gp])
              for gp in pairs]
    u_st = [_dot(tinv[gp].astype(BF16), rhs_st[gp].astype(BF16)) for gp in pairs]
    y_st = [_dot(jnp.where(jnp.concatenate([m_incl, m_incl], axis=1), q4[gp][LANES:], 0.0).astype(BF16),
                 jnp.concatenate([u_st[gp].astype(BF16), v_st[gp]], axis=0)) for gp in pairs]
    ys = [ar[gp][L:] + y_st[gp][0:L] + y_st[gp][L:] for gp in pairs]
    for gp in pairs:
        uv = jnp.concatenate([u_st[gp][0:L] + u_st[gp][L:], vv[gp]], axis=0)
        bk = jnp.concatenate([bw[:, sls[gp]], kw[:, sls[gp]]], axis=0)
        s_add = _dot(uv.T.astype(BF16), bk.astype(BF16))
        state_ref[gp] = s0[gp] * wl[:, sls[gp]] + jnp.where(same, s_add, 0.0)
    y = jnp.concatenate(ys, axis=1)

    inv_n = 1.0 / HEAD_DIM
    d = y - _dot_split(y, ones_bd) * inv_n
    var = _dot_split(d * d, ones_bd) * inv_n
    yn = d * lax.rsqrt(var + LNX_EPS) * lnw_ref[...] + lnb_ref[...]
    bonus = _dot_split(r * kmod * rk_ref[...], ones_bd) * v
    y_ref[...] = ((yn + bonus) * g).astype(BF16)


def _rwkv(p_rw, batch, seq, mu, w0, a0, w_wa, g_up, k_k, k_a, r_k, lnx_w, lnx_b, tri, ones_bd):
    n = batch * seq
    nc = seq // CHUNK
    vec = lambda: _const_spec((1, RW_DIM))
    return pl.pallas_call(
        _rwkv_body,
        grid=(batch, nc),
        in_specs=[
            pl.BlockSpec((CHUNK, RW_COLS), lambda b, c: (b * nc + c, 0)),
            _const_spec((1, RW_COLS)), vec(), vec(),
            _const_spec((LANES, 2 * RW_DIM)), _const_spec((LANES, RW_DIM)),
            vec(), vec(), vec(), vec(), vec(),
            _const_spec((CHUNK, CHUNK)), _const_spec((RW_DIM, RW_DIM)),
        ],
        out_specs=pl.BlockSpec((CHUNK, RW_DIM), lambda b, c: (b * nc + c, 0)),
        out_shape=jax.ShapeDtypeStruct((n, RW_DIM), BF16),
        scratch_shapes=[
            pltpu.VMEM((1, RW_COLS), F32),
            pltpu.VMEM((RW_DIM // LANES, LANES, LANES), F32),
        ],
        compiler_params=_params("arbitrary", "arbitrary"),
        name="rwkv",
    )(p_rw, mu, w0, a0, w_wa, g_up, k_k, k_a, r_k, lnx_w, lnx_b, tri, ones_bd)


def _bucket_ranges():
    max_exact = NUM_BUCKETS // 2
    ratio_num, ratio_den = REL_MAX_DIST, max_exact
    bounds = []
    for b in range(NUM_BUCKETS):
        if b < max_exact:
            bounds.append((b, b + 1))
            continue
        def first(j):
            n = max_exact
            while (n ** (NUM_BUCKETS - max_exact)) * (ratio_den ** j) < (ratio_num ** j) * (max_exact ** (NUM_BUCKETS - max_exact)):
                n += 1
            return n
        lo = first(b - max_exact)
        hi = first(b + 1 - max_exact) if b < NUM_BUCKETS - 1 else 1 << 30
        bounds.append((lo, hi))
    return bounds


def _t5bias_body(tab_ref, o_ref, *, ranges):
    h = pl.program_id(0)
    blk = MOBA_BLOCK
    rel = lax.broadcasted_iota(jnp.int32, (blk, blk), 1) - lax.broadcasted_iota(jnp.int32, (blk, blk), 0)
    nblk = o_ref.shape[1] // blk
    for di in range(nblk):
        dist = rel + di * blk
        n = jnp.maximum(dist, 0)
        acc = jnp.zeros((blk, blk), F32)
        for b, (lo, hi) in enumerate(ranges):
            if hi <= di * blk - (blk - 1) or lo > di * blk + (blk - 1):
                continue
            acc = jnp.where((n >= lo) & (n < hi), tab_ref[h, b], acc)
        o_ref[0, (nblk - 1 - di) * blk:(nblk - di) * blk, :] = jnp.where(dist >= 0, acc, NEG_INF)


def _t5bias(table, nblk):
    heads = table.shape[0]
    return pl.pallas_call(
        functools.partial(_t5bias_body, ranges=_bucket_ranges()),
        grid=(heads,),
        in_specs=[pl.BlockSpec(memory_space=pltpu.SMEM)],
        out_specs=pl.BlockSpec((1, nblk * MOBA_BLOCK, MOBA_BLOCK), lambda h: (h, 0, 0)),
        out_shape=jax.ShapeDtypeStruct((heads, nblk * MOBA_BLOCK, MOBA_BLOCK), F32),
        compiler_params=_params("arbitrary"),
        name="t5bias",
    )(table)


def _moba_body(k_ref, qt_ref, vt_ref, bias_ref, qg_ref, kg_ref, ones_ref, o_ref, kaug_ref, qaug_ref, ot_ref):
    seq = k_ref.shape[0]
    blk = MOBA_BLOCK
    nb = seq // blk

    k2 = k_ref[...].astype(F32)
    ms = _dot_split(k2 * k2, ones_ref[...]) * (1.0 / HEAD_DIM)
    knf = k2 * lax.rsqrt(ms + NORM_EPS) * kg_ref[...]
    kmean = jnp.concatenate(
        [jnp.mean(knf[j * blk:(j + 1) * blk], axis=0, keepdims=True) for j in range(nb)], axis=0)
    klane = lax.broadcasted_iota(jnp.int32, (seq, LANES), 1)
    kblk = lax.broadcasted_iota(jnp.int32, (seq, LANES), 0) // blk
    kaug_ref[0] = jnp.where(klane < HEAD_DIM, knf, jnp.where(klane - HEAD_DIM == kblk, 1.0, 0.0)).astype(BF16)
    kaug_ref[1] = jnp.where(klane >= HEAD_DIM, knf, jnp.where(klane == kblk, 1.0, 0.0)).astype(BF16)

    q2 = qt_ref[...].astype(F32)

    def nrm(qh):
        return qh * lax.rsqrt(jnp.mean(qh * qh, axis=0, keepdims=True) + NORM_EPS)

    qn = jnp.concatenate([nrm(q2[0:HEAD_DIM]), nrm(q2[HEAD_DIM:])], axis=0) * qg_ref[...] * (HEAD_DIM ** -0.5)
    qnb = qn.astype(BF16)

    km_hi = kmean.astype(BF16)
    km_lo = (kmean - km_hi.astype(F32)).astype(BF16)
    lane = lax.broadcasted_iota(jnp.int32, (nb, LANES), 1)
    rowi = lax.broadcasted_iota(jnp.int32, (nb, seq), 0)
    qblk = lax.broadcasted_iota(jnp.int32, (nb, seq), 1) // blk
    past = rowi < qblk
    qblk1 = qblk[0:1]
    zero_b = jnp.zeros_like(km_hi)
    pad = jnp.zeros((HEAD_DIM - nb, seq), F32)

    for hh in range(2):
        hm = (lane < HEAD_DIM) if hh == 0 else (lane >= HEAD_DIM)
        gate = _dot(jnp.where(hm, km_hi, zero_b), qnb) + _dot(jnp.where(hm, km_lo, zero_b), qnb)
        rows = []
        for j in range(nb):
            gj = gate[j:j + 1, :]
            beats = (gate > gj) | ((gate == gj) & (rowi < j))
            cnt = jnp.sum(jnp.where(past & beats, 1.0, 0.0), axis=0, keepdims=True)
            sel = ((qblk1 > j) & (cnt < MOBA_TOPK)) | (qblk1 == j)
            rows.append(jnp.where(sel, 0.0, NEG_INF))
        selb = jnp.concatenate(rows, axis=0)
        if hh == 0:
            qaug_ref[hh] = jnp.concatenate([qn[0:HEAD_DIM], selb, pad], axis=0).astype(BF16)
        else:
            qaug_ref[hh] = jnp.concatenate([selb, pad, qn[HEAD_DIM:]], axis=0).astype(BF16)

    for hh in range(2):
        hrows = slice(hh * HEAD_DIM, (hh + 1) * HEAD_DIM)
        for i in range(nb):
            nk = (i + 1) * blk
            qcols = slice(i * blk, (i + 1) * blk)
            s = _dot(kaug_ref[hh, 0:nk, :], qaug_ref[hh, :, qcols]) + bias_ref[hh, (nb - 1 - i) * blk:, :]
            p = jnp.exp(s - jnp.max(s, axis=0, keepdims=True))
            l = jnp.sum(p, axis=0, keepdims=True)
            acc = _dot(vt_ref[hrows, 0:nk], p.astype(BF16))
            ot_ref[hrows, qcols] = acc / l

    for i in range(nb):
        o_ref[i * blk:(i + 1) * blk, :] = ot_ref[:, i * blk:(i + 1) * blk].T.astype(BF16)


def _moba(k_att, q_t, v_t, bias_t, qg, kg, ones_bd, batch, seq):
    n = batch * seq
    npair = ATT_DIM // LANES
    nb = seq // MOBA_BLOCK
    return pl.pallas_call(
        _moba_body,
        grid=(npair, batch),
        in_specs=[
            pl.BlockSpec((seq, LANES), lambda g, b: (b, g)),
            pl.BlockSpec((LANES, seq), lambda g, b: (g, b)),
            pl.BlockSpec((LANES, seq), lambda g, b: (g, b)),
            pl.BlockSpec((2, seq, MOBA_BLOCK), lambda g, b: (g, 0, 0)),
            _const_spec((LANES, 1)), _const_spec((1, LANES)), _const_spec((LANES, LANES)),
        ],
        out_specs=pl.BlockSpec((seq, LANES), lambda g, b: (b, g)),
        out_shape=jax.ShapeDtypeStruct((n, ATT_DIM), BF16),
        scratch_shapes=[
            pltpu.VMEM((2, seq, LANES), BF16),
            pltpu.VMEM((2, LANES, seq), BF16),
            pltpu.VMEM((LANES, seq), F32),
        ],
        compiler_params=_params("arbitrary", "arbitrary"),
        name="moba",
    )(k_att, q_t, v_t, bias_t, qg, kg, ones_bd)


def _outproj_body(x_ref, yr_ref, ya_ref, wr_ref, wa_ref, o_ref):
    o_ref[...] = x_ref[...] + _dot(yr_ref[...], wr_ref[...]) + _dot(ya_ref[...], wa_ref[...])


def _outproj(x2, y_rw, y_att, w_r, w_a, tm=1024):
    n = x2.shape[0]
    return pl.pallas_call(
        _outproj_body,
        grid=(n // tm,),
        in_specs=[
            pl.BlockSpec((tm, D_MODEL), lambda i: (i, 0)),
            pl.BlockSpec((tm, RW_DIM), lambda i: (i, 0)),
            pl.BlockSpec((tm, ATT_DIM), lambda i: (i, 0)),
            _const_spec((RW_DIM, D_MODEL)), _const_spec((ATT_DIM, D_MODEL)),
        ],
        out_specs=pl.BlockSpec((tm, D_MODEL), lambda i: (i, 0)),
        out_shape=jax.ShapeDtypeStruct((n, D_MODEL), F32),
        compiler_params=_params("arbitrary"),
        name="outproj",
    )(x2, y_rw, y_att, w_r, w_a)


def _causal_conv3(h, carry_ref, cols, w):
    tm = h.shape[0]
    row = lax.broadcasted_iota(jnp.int32, h.shape, 0)
    c = carry_ref[:, cols]
    h1 = jnp.where(row == 0, c[7:8], pltpu.roll(h, 1, axis=0))
    h2 = jnp.where(row == 0, c[6:7], jnp.where(row == 1, c[7:8], pltpu.roll(h, 2, axis=0)))
    carry_ref[:, cols] = h[tm - 8:tm]
    return w[2:3] * h + w[1:2] * h1 + w[0:1] * h2


def _ffn_body(x_ref, g_ref, up_ref, cw_ref, cb_ref, down_ref, o_ref, xn_ref, carry_ref, *, tc):
    @pl.when(pl.program_id(1) == 0)
    def _():
        carry_ref[...] = jnp.zeros_like(carry_ref)

    x = x_ref[...]
    xn_ref[...] = _rms_norm_rows(x, g_ref[...]).astype(BF16)
    o_ref[...] = x

    def branch(c0):
        cols = pl.ds(c0, tc)
        h = _dot(xn_ref[...], up_ref[:, cols])
        return _causal_conv3(h, carry_ref, cols, cw_ref[:, cols]) + cb_ref[:, cols]

    def chunk(ci, carry):
        c0 = pl.multiple_of(ci * tc, tc)
        za = branch(c0)
        zu = branch(pl.multiple_of(D_FF + c0, LANES))
        z = (za * jax.nn.sigmoid(za) * zu).astype(BF16)
        o_ref[...] += _dot(z, down_ref[pl.ds(c0, tc), :])
        return carry

    lax.fori_loop(0, D_FF // tc, chunk, 0)


def _ffn(x2, batch, seq, g, up, conv_w, conv_b, down, tm=512, tc=256):
    n = batch * seq
    nt = seq // tm
    return pl.pallas_call(
        functools.partial(_ffn_body, tc=tc),
        grid=(batch, nt),
        in_specs=[
            pl.BlockSpec((tm, D_MODEL), lambda b, t: (b * nt + t, 0)),
            _const_spec((1, D_MODEL)),
            _const_spec((D_MODEL, 2 * D_FF)),
            _const_spec((3, 2 * D_FF)),
            _const_spec((1, 2 * D_FF)),
            _const_spec((D_FF, D_MODEL)),
        ],
        out_specs=pl.BlockSpec((tm, D_MODEL), lambda b, t: (b * nt + t, 0)),
        out_shape=jax.ShapeDtypeStruct((n, D_MODEL), F32),
        scratch_shapes=[pltpu.VMEM((tm, D_MODEL), BF16), pltpu.VMEM((8, 2 * D_FF), F32)],
        compiler_params=_params("arbitrary", "arbitrary"),
        name="ffn",
    )(x2, g, up, conv_w, conv_b, down)


def _sconv_body(x_ref, g_ref, win_ref, cw_ref, wout_ref, o_ref, xn_ref, carry_ref, *, tc):
    @pl.when(pl.program_id(1) == 0)
    def _():
        carry_ref[...] = jnp.zeros_like(carry_ref)

    x = x_ref[...]
    xn_ref[...] = _rms_norm_rows(x, g_ref[...]).astype(BF16)
    o_ref[...] = x

    def chunk(ci, carry):
        c0 = pl.multiple_of(ci * tc, tc)
        cols = pl.ds(c0, tc)
        xn = xn_ref[...]
        b_gate = _dot(xn, win_ref[:, cols])
        c_gate = _dot(xn, win_ref[:, pl.ds(pl.multiple_of(D_MODEL + c0, tc), tc)])
        hid = _dot(xn, win_ref[:, pl.ds(pl.multiple_of(2 * D_MODEL + c0, tc), tc)])
        conv = _causal_conv3(c_gate * hid, carry_ref, cols, cw_ref[:, cols])
        o_ref[...] += _dot((b_gate * conv).astype(BF16), wout_ref[cols, :])
        return carry

    lax.fori_loop(0, D_MODEL // tc, chunk, 0)


def _sconv(x2, batch, seq, g, w_in, conv_w, w_out, tm=512, tc=256):
    n = batch * seq
    nt = seq // tm
    return pl.pallas_call(
        functools.partial(_sconv_body, tc=tc),
        grid=(batch, nt),
        in_specs=[
            pl.BlockSpec((tm, D_MODEL), lambda b, t: (b * nt + t, 0)),
            _const_spec((1, D_MODEL)),
            _const_spec((D_MODEL, 3 * D_MODEL)),
            _const_spec((3, D_MODEL)),
            _const_spec((D_MODEL, D_MODEL)),
        ],
        out_specs=pl.BlockSpec((tm, D_MODEL), lambda b, t: (b * nt + t, 0)),
        out_shape=jax.ShapeDtypeStruct((n, D_MODEL), F32),
        scratch_shapes=[pltpu.VMEM((tm, D_MODEL), BF16), pltpu.VMEM((8, D_MODEL), F32)],
        compiler_params=_params("arbitrary", "arbitrary"),
        name="sconv",
    )(x2, g, w_in, conv_w, w_out)


def _block_diag_ones(n):
    i = jnp.arange(n) // HEAD_DIM
    return (i[:, None] == i[None, :]).astype(BF16)


def _rwkv_moba_mixer(x2, batch, seq, rel_bias, mix_norm, w_in, shift_mu, w0, w_lora_up, a0, a_lora_up, g_lora_up,
                     k_k, k_a, r_k, lnx_w, lnx_b, q_norm, k_norm, w_out):
    row = lambda z: z.reshape(1, -1).astype(F32)
    w_in = w_in.astype(BF16)
    q0 = RW_COLS
    w_rw = w_in[:, :q0]
    w_qt = w_in[:, q0:q0 + ATT_DIM].T
    w_k = w_in[:, q0 + ATT_DIM:q0 + 2 * ATT_DIM]
    w_vt = w_in[:, q0 + 2 * ATT_DIM:].T
    p_rw, k_att, q_t, v_t = _inproj0(x2, row(mix_norm), w_rw, w_k, w_qt, w_vt)

    zeros = jnp.zeros_like(w_lora_up)
    w_wa = jnp.concatenate([jnp.concatenate([w_lora_up, zeros], axis=1),
                            jnp.concatenate([zeros, a_lora_up], axis=1)], axis=0).astype(BF16)
    tri = (jnp.arange(CHUNK)[:, None] >= jnp.arange(CHUNK)[None, :]).astype(BF16)
    y_rw = _rwkv(p_rw, batch, seq, row(shift_mu), row(w0), row(a0), w_wa, g_lora_up.astype(BF16),
                 row(k_k), row(k_a), row(r_k), row(lnx_w), row(lnx_b), tri, _block_diag_ones(RW_DIM))

    bias_t = _t5bias(rel_bias.T.astype(F32), seq // MOBA_BLOCK)
    qg = jnp.tile(q_norm.astype(F32), 2).reshape(LANES, 1)
    kg = jnp.tile(k_norm.astype(F32), 2).reshape(1, LANES)
    y_att = _moba(k_att, q_t, v_t, bias_t, qg, kg, _block_diag_ones(LANES), batch, seq)

    w_out = w_out.astype(BF16)
    return _outproj(x2, y_rw, y_att, w_out[:RW_DIM], w_out[RW_DIM:])


def kernel(x, rel_bias, l0_mix_norm, l0_w_in, l0_shift_mu, l0_w0, l0_w_lora_up, l0_a0, l0_a_lora_up, l0_g_lora_up, l0_k_k, l0_k_a, l0_r_k, l0_lnx_w, l0_lnx_b, l0_q_norm, l0_k_norm, l0_w_out, l0_ffn_norm, l0_ffn_up, l0_ffn_conv_w, l0_ffn_conv_b, l0_ffn_down, l1_mix_norm, l1_w_in, l1_conv_w, l1_w_out, l1_ffn_norm, l1_ffn_up, l1_ffn_conv_w, l1_ffn_conv_b, l1_ffn_down):
    batch, seq, d = x.shape
    row = lambda z: z.reshape(1, -1).astype(F32)
    x2 = x.reshape(batch * seq, d)
    x2 = _rwkv_moba_mixer(x2, batch, seq, rel_bias, l0_mix_norm, l0_w_in, l0_shift_mu, l0_w0, l0_w_lora_up, l0_a0,
                          l0_a_lora_up, l0_g_lora_up, l0_k_k, l0_k_a, l0_r_k, l0_lnx_w, l0_lnx_b, l0_q_norm,
                          l0_k_norm, l0_w_out)
    x2 = _ffn(x2, batch, seq, row(l0_ffn_norm), l0_ffn_up.astype(BF16), l0_ffn_conv_w.astype(F32),
              row(l0_ffn_conv_b), l0_ffn_down.astype(BF16))
    x2 = _sconv(x2, batch, seq, row(l1_mix_norm), l1_w_in.astype(BF16), l1_conv_w.astype(F32),
                l1_w_out.astype(BF16))
    x2 = _ffn(x2, batch, seq, row(l1_ffn_norm), l1_ffn_up.astype(BF16), l1_ffn_conv_w.astype(F32),
              row(l1_ffn_conv_b), l1_ffn_down.astype(BF16))
    return x2.reshape(batch, seq, d)
```

```python
import functools
import math

import jax
import jax.numpy as jnp
from jax import lax
from jax.experimental import pallas as pl
from jax.experimental.pallas import tpu as pltpu

F32 = jnp.float32
BF16 = jnp.bfloat16

D_MODEL = 1024
HEAD_DIM = 64
RW_DIM = 512
RW_COLS = 3 * RW_DIM + 64 + 64 + 128
ATT_DIM = 512
MOBA_BLOCK = 256
MOBA_TOPK = 3
NUM_BUCKETS = 32
REL_MAX_DIST = 1024
D_FF = 2816
NORM_EPS = 1e-6
LNX_EPS = 64e-5
NEG_INF = -1e30
LOG2E = math.log2(math.e)

CHUNK = 64
LANES = 128
VMEM_LIMIT = 52 * 1024 * 1024


def _dot(a, b):
    return jnp.dot(a, b, preferred_element_type=F32)


def _dot_nt(a, b):
    return lax.dot_general(a, b, (((1,), (1,)), ((), ())), preferred_element_type=F32)


def _dot_split(x, w):
    hi = x.astype(BF16)
    lo = (x - hi.astype(F32)).astype(BF16)
    return _dot(hi, w) + _dot(lo, w)


def _rms_norm_rows(x, g):
    return x * lax.rsqrt(jnp.mean(x * x, axis=-1, keepdims=True) + NORM_EPS) * g


def _const_spec(shape):
    return pl.BlockSpec(shape, lambda *_: (0,) * len(shape), pipeline_mode=pl.Buffered(1))


def _params(*sem):
    return pltpu.CompilerParams(dimension_semantics=sem, vmem_limit_bytes=VMEM_LIMIT)


def _inproj0_body(x_ref, g_ref, wrw_ref, wk_ref, wqt_ref, wvt_ref, prw_ref, k_ref, qt_ref, vt_ref):
    xn = _rms_norm_rows(x_ref[...], g_ref[...]).astype(BF16)
    prw_ref[...] = _dot(xn, wrw_ref[...])
    k_ref[...] = _dot(xn, wk_ref[...]).astype(BF16)
    qt_ref[...] = _dot_nt(wqt_ref[...], xn).astype(BF16)
    vt_ref[...] = _dot_nt(wvt_ref[...], xn).astype(BF16)


def _inproj0(x2, g, w_rw, w_k, w_qt, w_vt, tm=512):
    n = x2.shape[0]
    return pl.pallas_call(
        _inproj0_body,
        grid=(n // tm,),
        in_specs=[
            pl.BlockSpec((tm, D_MODEL), lambda i: (i, 0)),
            _const_spec((1, D_MODEL)),
            _const_spec((D_MODEL, RW_COLS)),
            _const_spec((D_MODEL, ATT_DIM)),
            _const_spec((ATT_DIM, D_MODEL)),
            _const_spec((ATT_DIM, D_MODEL)),
        ],
        out_specs=[
            pl.BlockSpec((tm, RW_COLS), lambda i: (i, 0)),
            pl.BlockSpec((tm, ATT_DIM), lambda i: (i, 0)),
            pl.BlockSpec((ATT_DIM, tm), lambda i: (0, i)),
            pl.BlockSpec((ATT_DIM, tm), lambda i: (0, i)),
        ],
        out_shape=[
            jax.ShapeDtypeStruct((n, RW_COLS), F32),
            jax.ShapeDtypeStruct((n, ATT_DIM), BF16),
            jax.ShapeDtypeStruct((ATT_DIM, n), BF16),
            jax.ShapeDtypeStruct((ATT_DIM, n), BF16),
        ],
        compiler_params=_params("arbitrary"),
        name="inproj0",
    )(x2, g, w_rw, w_k, w_qt, w_vt)


def _rwkv_body(p_ref, mu_ref, w0_ref, a0_ref, wwa_ref, gup_ref, kk_ref, ka_ref, rk_ref, lnw_ref, lnb_ref,
               tri_ref, ones_ref, y_ref, carry_ref, state_ref):
    L = CHUNK
    ns = p_ref.shape[0]
    rows = [slice(s * L, (s + 1) * L) for s in range(ns)]

    @pl.when(pl.program_id(1) == 0)
    def _():
        carry_ref[...] = jnp.zeros_like(carry_ref)
        state_ref[...] = jnp.zeros_like(state_ref)

    p = p_ref[...].reshape(ns * L, RW_COLS)
    row = lax.broadcasted_iota(jnp.int32, p.shape, 0)
    prev = pltpu.roll(p, 1, axis=0)
    for s in range(ns):
        prev = jnp.where(row == s * L, carry_ref[s:s + 1, :], prev)
    for s in range(ns):
        carry_ref[s:s + 1, :] = p[(s + 1) * L - 1:(s + 1) * L, :]
    ps = p + (prev - p) * mu_ref[...]
    r = ps[:, 0:RW_DIM]
    k = ps[:, RW_DIM:2 * RW_DIM]
    v = ps[:, 2 * RW_DIM:3 * RW_DIM]
    slab = ps[:, 3 * RW_DIM:3 * RW_DIM + LANES]
    dg = ps[:, 3 * RW_DIM + LANES:RW_COLS]
    lo = lax.broadcasted_iota(jnp.int32, (L, LANES), 1) < HEAD_DIM
    lo_all = lax.broadcasted_iota(jnp.int32, slab.shape, 1) < HEAD_DIM

    wa = _dot(jnp.where(lo_all, jnp.tanh(slab), slab).astype(BF16), wwa_ref[...])
    ld = -math.exp(-0.5) * jax.nn.sigmoid(w0_ref[...] + wa[:, :RW_DIM])
    a = jax.nn.sigmoid(a0_ref[...] + wa[:, RW_DIM:])
    g = _dot(jax.nn.sigmoid(dg).astype(BF16), gup_ref[...])
    ones_bd = ones_ref[...]
    kk = k * kk_ref[...]
    kk = kk * lax.rsqrt(jnp.maximum(_dot_split(kk * kk, ones_bd), 1e-24))
    kmod = k * (1.0 + (a - 1.0) * ka_ref[...])

    tri = tri_ref[...]
    ld_hi = ld.astype(BF16)
    rem = ld - ld_hi.astype(F32)
    ld_mid = rem.astype(BF16)
    ld_lo = (rem - ld_mid.astype(F32)).astype(BF16)
    cs = _dot(tri, ld_hi) + _dot(tri, ld_mid) + _dot(tri, ld_lo)
    e_c = jnp.exp(cs)
    e_nc = jnp.exp(-cs)
    abar = -kk * jnp.exp(cs - ld)
    rbar = r * e_c
    bbar = kk * a * e_nc
    kbar = kmod * e_nc
    wl = [e_c[(s + 1) * L - 1:(s + 1) * L, :] for s in range(ns)]

    ri = lax.broadcasted_iota(jnp.int32, (LANES, LANES), 0)
    ci = lax.broadcasted_iota(jnp.int32, (LANES, LANES), 1)
    same = (ri >= HEAD_DIM) == (ci >= HEAD_DIM)
    t_i = ri & (L - 1)
    s_i = ci & (L - 1)
    m_strict = same & (s_i < t_i)
    m_incl = same & (s_i <= t_i)
    eye = ri == ci

    def stack(z):
        return jnp.concatenate([jnp.where(lo, z, 0.0), jnp.where(lo, 0.0, z)], axis=0)

    npair = RW_DIM // LANES
    pairs = range(ns * npair)
    sls = [(rows[n // npair], slice(LANES * (n % npair), LANES * (n % npair + 1))) for n in pairs]
    ab = [abar[sl] for sl in sls]
    rb = [rbar[sl] for sl in sls]
    vv = [v[sl] for sl in sls]
    s0 = [state_ref[gp] for gp in pairs]
    q4 = [_dot_nt(jnp.concatenate([stack(ab[gp]), stack(rb[gp])], axis=0).astype(BF16),
                  jnp.concatenate([bbar[sls[gp]]] * 2 + [kbar[sls[gp]]] * 2, axis=0).astype(BF16))
          for gp in pairs]
    ar = [_dot_nt(jnp.concatenate([ab[gp], rb[gp]], axis=0).astype(BF16), s0[gp].astype(BF16)) for gp in pairs]
    a_ab = [jnp.where(m_strict, q[0:LANES, 0:LANES], 0.0) for q in q4]
    tinv = [jnp.where(eye, 1.0, a) for a in a_ab]
    pw = []
    for a in a_ab:
        a16 = a.astype(BF16)
        pw.append(_dot(a16, a16).astype(BF16))
    for _ in range(4):
        both = [_dot(jnp.concatenate([tinv[gp].astype(BF16), pw[gp]], axis=0), pw[gp]) for gp in pairs]
        tinv = [tinv[gp] + both[gp][0:LANES] for gp in pairs]
        pw = [both[gp][LANES:].astype(BF16) for gp in pairs]
    tinv = [tinv[gp] + _dot(tinv[gp].astype(BF16), pw[gp]) for gp in pairs]
    v_st = [stack(z).astype(BF16) for z in vv]
    rhs_st = [stack(ar[gp][0:L]) + _dot(jnp.where(m_strict, q4[gp][0:LANES, LANES:], 0.0).astype(BF16), v_st[gp])
              for gp in pairs]
    u_st = [_dot(tinv[gp].astype(BF16), rhs_st[gp].astype(BF16)) for gp in pairs]
    y_st = [_dot(jnp.where(jnp.concatenate([m_incl, m_incl], axis=1), q4[gp][LANES:], 0.0).astype(BF16),
                 jnp.concatenate([u_st[gp].astype(BF16), v_st[gp]], axis=0)) for gp in pairs]
    ys = [ar[gp][L:] + y_st[gp][0:L] + y_st[gp][L:] for gp in pairs]
    for gp in pairs:
        wl_gp = wl[gp // npair][:, sls[gp][1]]
        uv = jnp.concatenate([u_st[gp][0:L] + u_st[gp][L:], vv[gp]], axis=0)
        bk = jnp.concatenate([bbar[sls[gp]], kbar[sls[gp]]], axis=0) * wl_gp
        s_add = _dot(uv.T.astype(BF16), bk.astype(BF16))
        state_ref[gp] = s0[gp] * wl_gp + jnp.where(same, s_add, 0.0)
    y = jnp.concatenate([jnp.concatenate(ys[s * npair:(s + 1) * npair], axis=1) for s in range(ns)], axis=0)

    inv_n = 1.0 / HEAD_DIM
    d = y - _dot_split(y, ones_bd) * inv_n
    var = _dot_split(d * d, ones_bd) * inv_n
    yn = d * lax.rsqrt(var + LNX_EPS) * lnw_ref[...] + lnb_ref[...]
    bonus = _dot_split(r * kmod * rk_ref[...], ones_bd) * v
    y_ref[...] = ((yn + bonus) * g).astype(BF16).reshape(ns, L, RW_DIM)


def _rwkv(p_rw, batch, seq, mu, w0, a0, w_wa, g_up, k_k, k_a, r_k, lnx_w, lnx_b, ones_bd, ns=2):
    nc = seq // CHUNK
    vec = lambda: _const_spec((1, RW_DIM))
    idx = jnp.arange(ns * CHUNK)
    tri = ((idx[:, None] >= idx[None, :]) & (idx[:, None] // CHUNK == idx[None, :] // CHUNK)).astype(BF16)
    y = pl.pallas_call(
        _rwkv_body,
        grid=(batch // ns, nc),
        in_specs=[
            pl.BlockSpec((ns, CHUNK, RW_COLS), lambda b, c: (b, c, 0)),
            _const_spec((1, RW_COLS)), vec(), vec(),
            _const_spec((LANES, 2 * RW_DIM)), _const_spec((LANES, RW_DIM)),
            vec(), vec(), vec(), vec(), vec(),
            _const_spec((ns * CHUNK, ns * CHUNK)), _const_spec((RW_DIM, RW_DIM)),
        ],
        out_specs=pl.BlockSpec((ns, CHUNK, RW_DIM), lambda b, c: (b, c, 0)),
        out_shape=jax.ShapeDtypeStruct((batch, seq, RW_DIM), BF16),
        scratch_shapes=[
            pltpu.VMEM((8, RW_COLS), F32),
            pltpu.VMEM((ns * RW_DIM // LANES, LANES, LANES), F32),
        ],
        compiler_params=_params("arbitrary", "arbitrary"),
        name="rwkv",
    )(p_rw.reshape(batch, seq, RW_COLS), mu, w0, a0, w_wa, g_up, k_k, k_a, r_k, lnx_w, lnx_b, tri, ones_bd)
    return y.reshape(batch * seq, RW_DIM)


def _bucket_ranges():
    max_exact = NUM_BUCKETS // 2
    ratio_num, ratio_den = REL_MAX_DIST, max_exact
    bounds = []
    for b in range(NUM_BUCKETS):
        if b < max_exact:
            bounds.append((b, b + 1))
            continue
        def first(j):
            n = max_exact
            while (n ** (NUM_BUCKETS - max_exact)) * (ratio_den ** j) < (ratio_num ** j) * (max_exact ** (NUM_BUCKETS - max_exact)):
                n += 1
            return n
        lo = first(b - max_exact)
        hi = first(b + 1 - max_exact) if b < NUM_BUCKETS - 1 else 1 << 30
        bounds.append((lo, hi))
    return bounds


def _t5bias_body(tab_ref, o_ref, *, ranges):
    h = pl.program_id(0)
    blk = MOBA_BLOCK
    rel = lax.broadcasted_iota(jnp.int32, (blk, blk), 1) - lax.broadcasted_iota(jnp.int32, (blk, blk), 0)
    nblk = o_ref.shape[1] // blk
    for di in range(nblk):
        dist = rel + di * blk
        n = jnp.maximum(dist, 0)
        acc = jnp.zeros((blk, blk), F32)
        for b, (lo, hi) in enumerate(ranges):
            if hi <= di * blk - (blk - 1) or lo > di * blk + (blk - 1):
                continue
            acc = jnp.where((n >= lo) & (n < hi), tab_ref[h, b] * LOG2E, acc)
        o_ref[0, (nblk - 1 - di) * blk:(nblk - di) * blk, :] = jnp.where(dist >= 0, acc, NEG_INF)


def _t5bias(table, nblk):
    heads = table.shape[0]
    return pl.pallas_call(
        functools.partial(_t5bias_body, ranges=_bucket_ranges()),
        grid=(heads,),
        in_specs=[pl.BlockSpec(memory_space=pltpu.SMEM)],
        out_specs=pl.BlockSpec((1, nblk * MOBA_BLOCK, MOBA_BLOCK), lambda h: (h, 0, 0)),
        out_shape=jax.ShapeDtypeStruct((heads, nblk * MOBA_BLOCK, MOBA_BLOCK), F32),
        compiler_params=_params("arbitrary"),
        name="t5bias",
    )(table)


def _moba_body(k_ref, qt_ref, vt_ref, bias_ref, qg_ref, kg_ref, ones_ref, o_ref, kaug_ref, qaug_ref, ot_ref):
    seq = k_ref.shape[0]
    blk = MOBA_BLOCK
    nb = seq // blk

    k2 = k_ref[...].astype(F32)
    ms = _dot_split(k2 * k2, ones_ref[...]) * (1.0 / HEAD_DIM)
    knf = k2 * lax.rsqrt(ms + NORM_EPS) * kg_ref[...]
    kmean = jnp.concatenate(
        [jnp.mean(knf[j * blk:(j + 1) * blk], axis=0, keepdims=True) for j in range(nb)], axis=0)
    klane = lax.broadcasted_iota(jnp.int32, (seq, LANES), 1)
    kblk = lax.broadcasted_iota(jnp.int32, (seq, LANES), 0) // blk
    kaug_ref[0] = jnp.where(klane < HEAD_DIM, knf, jnp.where(klane - HEAD_DIM == kblk, 1.0, 0.0)).astype(BF16)
    kaug_ref[1] = jnp.where(klane >= HEAD_DIM, knf, jnp.where(klane == kblk, 1.0, 0.0)).astype(BF16)

    q2 = qt_ref[...].astype(F32)

    def nrm(qh):
        return qh * lax.rsqrt(jnp.mean(qh * qh, axis=0, keepdims=True) + NORM_EPS)

    qn = jnp.concatenate([nrm(q2[0:HEAD_DIM]), nrm(q2[HEAD_DIM:])], axis=0) * qg_ref[...] \
        * (HEAD_DIM ** -0.5 * LOG2E)
    qnb = qn.astype(BF16)

    km_hi = kmean.astype(BF16)
    km_lo = (kmean - km_hi.astype(F32)).astype(BF16)
    lane = lax.broadcasted_iota(jnp.int32, (nb, LANES), 1)
    rowi = lax.broadcasted_iota(jnp.int32, (nb, seq), 0)
    qblk = lax.broadcasted_iota(jnp.int32, (nb, seq), 1) // blk
    past = rowi < qblk
    qblk1 = qblk[0:1]
    zero_b = jnp.zeros_like(km_hi)
    pad = jnp.zeros((HEAD_DIM - nb, seq), F32)

    for hh in range(2):
        hm = (lane < HEAD_DIM) if hh == 0 else (lane >= HEAD_DIM)
        gate = _dot(jnp.where(hm, km_hi, zero_b), qnb) + _dot(jnp.where(hm, km_lo, zero_b), qnb)
        rows = []
        for j in range(nb):
            gj = gate[j:j + 1, :]
            beats = (gate > gj) | ((gate == gj) & (rowi < j))
            cnt = jnp.sum(jnp.where(past & beats, 1.0, 0.0), axis=0, keepdims=True)
            sel = ((qblk1 > j) & (cnt < MOBA_TOPK)) | (qblk1 == j)
            rows.append(jnp.where(sel, 0.0, NEG_INF))
        selb = jnp.concatenate(rows, axis=0)
        if hh == 0:
            qaug_ref[hh] = jnp.concatenate([qn[0:HEAD_DIM], selb, pad], axis=0).astype(BF16)
        else:
            qaug_ref[hh] = jnp.concatenate([selb, pad, qn[HEAD_DIM:]], axis=0).astype(BF16)

    tiles = [(hh, i) for i in range(nb) for hh in range(2)]

    def scores(hh, i):
        nk = (i + 1) * blk
        return _dot(kaug_ref[hh, 0:nk, :], qaug_ref[hh, :, i * blk:(i + 1) * blk]) \
            + bias_ref[hh, (nb - 1 - i) * blk:, :]

    s_next = scores(*tiles[0])
    for n, (hh, i) in enumerate(tiles):
        s = s_next
        if n + 1 < len(tiles):
            s_next = scores(*tiles[n + 1])
        hrows = slice(hh * HEAD_DIM, (hh + 1) * HEAD_DIM)
        p = jnp.exp2(s - jnp.max(s, axis=0, keepdims=True))
        l = jnp.sum(p, axis=0, keepdims=True)
        acc = _dot(vt_ref[hrows, 0:(i + 1) * blk], p.astype(BF16))
        ot_ref[hrows, i * blk:(i + 1) * blk] = acc / l

    for i in range(nb):
        o_ref[i * blk:(i + 1) * blk, :] = ot_ref[:, i * blk:(i + 1) * blk].T.astype(BF16)


def _moba(k_att, q_t, v_t, bias_t, qg, kg, ones_bd, batch, seq):
    n = batch * seq
    npair = ATT_DIM // LANES
    nb = seq // MOBA_BLOCK
    return pl.pallas_call(
        _moba_body,
        grid=(npair, batch),
        in_specs=[
            pl.BlockSpec((seq, LANES), lambda g, b: (b, g)),
            pl.BlockSpec((LANES, seq), lambda g, b: (g, b)),
            pl.BlockSpec((LANES, seq), lambda g, b: (g, b)),
            pl.BlockSpec((2, seq, MOBA_BLOCK), lambda g, b: (g, 0, 0)),
            _const_spec((LANES, 1)), _const_spec((1, LANES)), _const_spec((LANES, LANES)),
        ],
        out_specs=pl.BlockSpec((seq, LANES), lambda g, b: (b, g)),
        out_shape=jax.ShapeDtypeStruct((n, ATT_DIM), BF16),
        scratch_shapes=[
            pltpu.VMEM((2, seq, LANES), BF16),
            pltpu.VMEM((2, LANES, seq), BF16),
            pltpu.VMEM((LANES, seq), F32),
        ],
        compiler_params=_params("arbitrary", "arbitrary"),
        name="moba",
    )(k_att, q_t, v_t, bias_t, qg, kg, ones_bd)


def _outproj_body(x_ref, yr_ref, ya_ref, wr_ref, wa_ref, o_ref):
    o_ref[...] = x_ref[...] + _dot(yr_ref[...], wr_ref[...]) + _dot(ya_ref[...], wa_ref[...])


def _outproj(x2, y_rw, y_att, w_r, w_a, tm=1024):
    n = x2.shape[0]
    return pl.pallas_call(
        _outproj_body,
        grid=(n // tm,),
        in_specs=[
            pl.BlockSpec((tm, D_MODEL), lambda i: (i, 0)),
            pl.BlockSpec((tm, RW_DIM), lambda i: (i, 0)),
            pl.BlockSpec((tm, ATT_DIM), lambda i: (i, 0)),
            _const_spec((RW_DIM, D_MODEL)), _const_spec((ATT_DIM, D_MODEL)),
        ],
        out_specs=pl.BlockSpec((tm, D_MODEL), lambda i: (i, 0)),
        out_shape=jax.ShapeDtypeStruct((n, D_MODEL), F32),
        compiler_params=_params("arbitrary"),
        name="outproj",
    )(x2, y_rw, y_att, w_r, w_a)


def _causal_conv3(h, carry_ref, cols, w):
    tm = h.shape[0]
    c = carry_ref[:, cols]
    row = lax.broadcasted_iota(jnp.int32, c.shape, 0)
    r1 = pltpu.roll(h, 1, axis=0)
    r2 = pltpu.roll(h, 2, axis=0)
    h1 = jnp.concatenate([jnp.where(row == 0, c[7:8], r1[0:8]), r1[8:]], axis=0)
    h2 = jnp.concatenate([jnp.where(row == 0, c[6:7], jnp.where(row == 1, c[7:8], r2[0:8])), r2[8:]], axis=0)
    carry_ref[:, cols] = h[tm - 8:tm]
    return w[2:3] * h + w[1:2] * h1 + w[0:1] * h2


def _ffn_body(x_ref, g_ref, up_ref, cw_ref, cb_ref, down_ref, o_ref, xn_ref, z_ref, carry_ref, *, tc):
    @pl.when(pl.program_id(1) == 0)
    def _():
        carry_ref[...] = jnp.zeros_like(carry_ref)

    xn_ref[...] = _rms_norm_rows(x_ref[...], g_ref[...]).astype(BF16)

    def branch(c0):
        cols = slice(c0, c0 + tc)
        h = _dot(xn_ref[...], up_ref[:, cols])
        return _causal_conv3(h, carry_ref, cols, cw_ref[:, cols]) + cb_ref[:, cols]

    for c0 in range(0, D_FF, tc):
        za = branch(c0)
        zu = branch(D_FF + c0)
        z_ref[:, c0:c0 + tc] = (za * jax.nn.sigmoid(za) * zu).astype(BF16)
    o_ref[...] = x_ref[...] + _dot(z_ref[...], down_ref[...])


def _ffn(x2, batch, seq, g, up, conv_w, conv_b, down, tm=512, tc=256):
    n = batch * seq
    nt = seq // tm
    return pl.pallas_call(
        functools.partial(_ffn_body, tc=tc),
        grid=(batch, nt),
        in_specs=[
            pl.BlockSpec((tm, D_MODEL), lambda b, t: (b * nt + t, 0)),
            _const_spec((1, D_MODEL)),
            _const_spec((D_MODEL, 2 * D_FF)),
            _const_spec((3, 2 * D_FF)),
            _const_spec((1, 2 * D_FF)),
            _const_spec((D_FF, D_MODEL)),
        ],
        out_specs=pl.BlockSpec((tm, D_MODEL), lambda b, t: (b * nt + t, 0)),
        out_shape=jax.ShapeDtypeStruct((n, D_MODEL), F32),
        scratch_shapes=[pltpu.VMEM((tm, D_MODEL), BF16), pltpu.VMEM((tm, D_FF), BF16),
                        pltpu.VMEM((8, 2 * D_FF), F32)],
        compiler_params=_params("arbitrary", "arbitrary"),
        name="ffn",
    )(x2, g, up, conv_w, conv_b, down)


def _sconv_body(x_ref, g_ref, win_ref, cw_ref, wout_ref, o_ref, xn_ref, z_ref, carry_ref, *, tc):
    @pl.when(pl.program_id(1) == 0)
    def _():
        carry_ref[...] = jnp.zeros_like(carry_ref)

    xn_ref[...] = _rms_norm_rows(x_ref[...], g_ref[...]).astype(BF16)

    for c0 in range(0, D_MODEL, tc):
        cols = slice(c0, c0 + tc)
        xn = xn_ref[...]
        b_gate = _dot(xn, win_ref[:, cols])
        c_gate = _dot(xn, win_ref[:, D_MODEL + c0:D_MODEL + c0 + tc])
        hid = _dot(xn, win_ref[:, 2 * D_MODEL + c0:2 * D_MODEL + c0 + tc])
        conv = _causal_conv3(c_gate * hid, carry_ref, cols, cw_ref[:, cols])
        z_ref[:, cols] = (b_gate * conv).astype(BF16)
    o_ref[...] = x_ref[...] + _dot(z_ref[...], wout_ref[...])


def _sconv(x2, batch, seq, g, w_in, conv_w, w_out, tm=512, tc=256):
    n = batch * seq
    nt = seq // tm
    return pl.pallas_call(
        functools.partial(_sconv_body, tc=tc),
        grid=(batch, nt),
        in_specs=[
            pl.BlockSpec((tm, D_MODEL), lambda b, t: (b * nt + t, 0)),
            _const_spec((1, D_MODEL)),
            _const_spec((D_MODEL, 3 * D_MODEL)),
            _const_spec((3, D_MODEL)),
            _const_spec((D_MODEL, D_MODEL)),
        ],
        out_specs=pl.BlockSpec((tm, D_MODEL), lambda b, t: (b * nt + t, 0)),
        out_shape=jax.ShapeDtypeStruct((n, D_MODEL), F32),
        scratch_shapes=[pltpu.VMEM((tm, D_MODEL), BF16), pltpu.VMEM((tm, D_MODEL), BF16),
                        pltpu.VMEM((8, D_MODEL), F32)],
        compiler_params=_params("arbitrary", "arbitrary"),
        name="sconv",
    )(x2, g, w_in, conv_w, w_out)


def _block_diag_ones(n):
    i = jnp.arange(n) // HEAD_DIM
    return (i[:, None] == i[None, :]).astype(BF16)


def _rwkv_moba_mixer(x2, batch, seq, rel_bias, mix_norm, w_in, shift_mu, w0, w_lora_up, a0, a_lora_up, g_lora_up,
                     k_k, k_a, r_k, lnx_w, lnx_b, q_norm, k_norm, w_out):
    row = lambda z: z.reshape(1, -1).astype(F32)
    w_in = w_in.astype(BF16)
    q0 = RW_COLS
    w_rw = w_in[:, :q0]
    w_qt = w_in[:, q0:q0 + ATT_DIM].T
    w_k = w_in[:, q0 + ATT_DIM:q0 + 2 * ATT_DIM]
    w_vt = w_in[:, q0 + 2 * ATT_DIM:].T
    p_rw, k_att, q_t, v_t = _inproj0(x2, row(mix_norm), w_rw, w_k, w_qt, w_vt)

    zeros = jnp.zeros_like(w_lora_up)
    w_wa = jnp.concatenate([jnp.concatenate([w_lora_up, zeros], axis=1),
                            jnp.concatenate([zeros, a_lora_up], axis=1)], axis=0).astype(BF16)
    y_rw = _rwkv(p_rw, batch, seq, row(shift_mu), row(w0), row(a0), w_wa, g_lora_up.astype(BF16),
                 row(k_k), row(k_a), row(r_k), row(lnx_w), row(lnx_b), _block_diag_ones(RW_DIM),
                 ns=4 if batch % 4 == 0 else 1)

    bias_t = _t5bias(rel_bias.T.astype(F32), seq // MOBA_BLOCK)
    qg = jnp.tile(q_norm.astype(F32), 2).reshape(LANES, 1)
    kg = jnp.tile(k_norm.astype(F32), 2).reshape(1, LANES)
    y_att = _moba(k_att, q_t, v_t, bias_t, qg, kg, _block_diag_ones(LANES), batch, seq)

    w_out = w_out.astype(BF16)
    return _outproj(x2, y_rw, y_att, w_out[:RW_DIM], w_out[RW_DIM:])


def kernel(x, rel_bias, l0_mix_norm, l0_w_in, l0_shift_mu, l0_w0, l0_w_lora_up, l0_a0, l0_a_lora_up, l0_g_lora_up, l0_k_k, l0_k_a, l0_r_k, l0_lnx_w, l0_lnx_b, l0_q_norm, l0_k_norm, l0_w_out, l0_ffn_norm, l0_ffn_up, l0_ffn_conv_w, l0_ffn_conv_b, l0_ffn_down, l1_mix_norm, l1_w_in, l1_conv_w, l1_w_out, l1_ffn_norm, l1_ffn_up, l1_ffn_conv_w, l1_ffn_conv_b, l1_ffn_down):
    batch, seq, d = x.shape
    row = lambda z: z.reshape(1, -1).astype(F32)
    x2 = x.reshape(batch * seq, d)
    x2 = _rwkv_moba_mixer(x2, batch, seq, rel_bias, l0_mix_norm, l0_w_in, l0_shift_mu, l0_w0, l0_w_lora_up, l0_a0,
                          l0_a_lora_up, l0_g_lora_up, l0_k_k, l0_k_a, l0_r_k, l0_lnx_w, l0_lnx_b, l0_q_norm,
                          l0_k_norm, l0_w_out)
    x2 = _ffn(x2, batch, seq, row(l0_ffn_norm), l0_ffn_up.astype(BF16), l0_ffn_conv_w.astype(F32),
              row(l0_ffn_conv_b), l0_ffn_down.astype(BF16))
    x2 = _sconv(x2, batch, seq, row(l1_mix_norm), l1_w_in.astype(BF16), l1_conv_w.astype(F32),
                l1_w_out.astype(BF16))
    x2 = _ffn(x2, batch, seq, row(l1_ffn_norm), l1_ffn_up.astype(BF16), l1_ffn_conv_w.astype(F32),
              row(l1_ffn_conv_b), l1_ffn_down.astype(BF16))
    return x2.reshape(batch, seq, d)
```

```python
import functools
import math

import jax
import jax.numpy as jnp
from jax import lax
from jax.experimental import pallas as pl
from jax.experimental.pallas import tpu as pltpu

F32 = jnp.float32
BF16 = jnp.bfloat16

D_MODEL = 1024
HEAD_DIM = 64
RW_DIM = 512
RW_COLS = 3 * RW_DIM + 64 + 64 + 128
ATT_DIM = 512
MOBA_BLOCK = 256
MOBA_TOPK = 3
NUM_BUCKETS = 32
REL_MAX_DIST = 1024
D_FF = 2816
NORM_EPS = 1e-6
LNX_EPS = 64e-5
NEG_INF = -1e30
LOG2E = math.log2(math.e)

CHUNK = 64
LANES = 128
VMEM_LIMIT = 52 * 1024 * 1024


def _dot(a, b):
    return jnp.dot(a, b, preferred_element_type=F32)


def _dot_nt(a, b):
    return lax.dot_general(a, b, (((1,), (1,)), ((), ())), preferred_element_type=F32)


def _dot_split(x, w):
    hi = x.astype(BF16)
    lo = (x - hi.astype(F32)).astype(BF16)
    return _dot(hi, w) + _dot(lo, w)


def _rms_norm_rows(x, g):
    return x * lax.rsqrt(jnp.mean(x * x, axis=-1, keepdims=True) + NORM_EPS) * g


def _const_spec(shape):
    return pl.BlockSpec(shape, lambda *_: (0,) * len(shape), pipeline_mode=pl.Buffered(1))


def _params(*sem):
    return pltpu.CompilerParams(dimension_semantics=sem, vmem_limit_bytes=VMEM_LIMIT)


def _inproj0_body(x_ref, g_ref, wrw_ref, wk_ref, wqt_ref, wvt_ref, prw_ref, k_ref, qt_ref, vt_ref):
    xn = _rms_norm_rows(x_ref[...], g_ref[...]).astype(BF16)
    prw_ref[...] = _dot(xn, wrw_ref[...])
    k_ref[...] = _dot(xn, wk_ref[...]).astype(BF16)
    qt_ref[...] = _dot_nt(wqt_ref[...], xn).astype(BF16)
    vt_ref[...] = _dot_nt(wvt_ref[...], xn).astype(BF16)


def _inproj0(x2, g, w_rw, w_k, w_qt, w_vt, tm=512):
    n = x2.shape[0]
    return pl.pallas_call(
        _inproj0_body,
        grid=(n // tm,),
        in_specs=[
            pl.BlockSpec((tm, D_MODEL), lambda i: (i, 0)),
            _const_spec((1, D_MODEL)),
            _const_spec((D_MODEL, RW_COLS)),
            _const_spec((D_MODEL, ATT_DIM)),
            _const_spec((ATT_DIM, D_MODEL)),
            _const_spec((ATT_DIM, D_MODEL)),
        ],
        out_specs=[
            pl.BlockSpec((tm, RW_COLS), lambda i: (i, 0)),
            pl.BlockSpec((tm, ATT_DIM), lambda i: (i, 0)),
            pl.BlockSpec((ATT_DIM, tm), lambda i: (0, i)),
            pl.BlockSpec((ATT_DIM, tm), lambda i: (0, i)),
        ],
        out_shape=[
            jax.ShapeDtypeStruct((n, RW_COLS), F32),
            jax.ShapeDtypeStruct((n, ATT_DIM), BF16),
            jax.ShapeDtypeStruct((ATT_DIM, n), BF16),
            jax.ShapeDtypeStruct((ATT_DIM, n), BF16),
        ],
        compiler_params=_params("arbitrary"),
        name="inproj0",
    )(x2, g, w_rw, w_k, w_qt, w_vt)


def _rwkv_body(p_ref, mu_ref, w0_ref, a0_ref, wwa_ref, gup_ref, kk_ref, ka_ref, rk_ref, lnw_ref, lnb_ref,
               tri_ref, ones_ref, y_ref, carry_ref, state_ref):
    L = CHUNK
    ns = p_ref.shape[0]
    rows = [slice(s * L, (s + 1) * L) for s in range(ns)]

    @pl.when(pl.program_id(1) == 0)
    def _():
        carry_ref[...] = jnp.zeros_like(carry_ref)
        state_ref[...] = jnp.zeros_like(state_ref)

    p = p_ref[...].reshape(ns * L, RW_COLS)
    row = lax.broadcasted_iota(jnp.int32, p.shape, 0)
    prev = pltpu.roll(p, 1, axis=0)
    for s in range(ns):
        prev = jnp.where(row == s * L, carry_ref[s:s + 1, :], prev)
    for s in range(ns):
        carry_ref[s:s + 1, :] = p[(s + 1) * L - 1:(s + 1) * L, :]
    ps = p + (prev - p) * mu_ref[...]
    r = ps[:, 0:RW_DIM]
    k = ps[:, RW_DIM:2 * RW_DIM]
    v = ps[:, 2 * RW_DIM:3 * RW_DIM]
    slab = ps[:, 3 * RW_DIM:3 * RW_DIM + LANES]
    dg = ps[:, 3 * RW_DIM + LANES:RW_COLS]
    lo = lax.broadcasted_iota(jnp.int32, (L, LANES), 1) < HEAD_DIM
    lo_all = lax.broadcasted_iota(jnp.int32, slab.shape, 1) < HEAD_DIM

    wa = _dot(jnp.where(lo_all, jnp.tanh(slab), slab).astype(BF16), wwa_ref[...])
    ld = -math.exp(-0.5) * jax.nn.sigmoid(w0_ref[...] + wa[:, :RW_DIM])
    a = jax.nn.sigmoid(a0_ref[...] + wa[:, RW_DIM:])
    g = _dot(jax.nn.sigmoid(dg).astype(BF16), gup_ref[...])
    ones_bd = ones_ref[...]
    kk = k * kk_ref[...]
    kk = kk * lax.rsqrt(jnp.maximum(_dot((kk * kk).astype(BF16), ones_bd), 1e-24))
    kmod = k * (1.0 + (a - 1.0) * ka_ref[...])

    tri = tri_ref[...]
    ld_hi = ld.astype(BF16)
    rem = ld - ld_hi.astype(F32)
    ld_mid = rem.astype(BF16)
    ld_lo = (rem - ld_mid.astype(F32)).astype(BF16)
    cs = _dot(tri, ld_hi) + _dot(tri, ld_mid) + _dot(tri, ld_lo)
    e_c = jnp.exp(cs)
    e_nc = jnp.exp(-cs)
    abar = -kk * jnp.exp(cs - ld)
    rbar = r * e_c
    bbar = kk * a * e_nc
    kbar = kmod * e_nc
    wl = [e_c[(s + 1) * L - 1:(s + 1) * L, :] for s in range(ns)]

    ri = lax.broadcasted_iota(jnp.int32, (LANES, LANES), 0)
    ci = lax.broadcasted_iota(jnp.int32, (LANES, LANES), 1)
    same = (ri >= HEAD_DIM) == (ci >= HEAD_DIM)
    t_i = lax.broadcasted_iota(jnp.int32, (L, LANES), 0)
    s_i = lax.broadcasted_iota(jnp.int32, (L, LANES), 1) & (L - 1)
    m_strict = s_i < t_i
    m_incl = s_i <= t_i
    m_incl2 = jnp.concatenate([m_incl, m_incl], axis=1)
    eye = s_i == t_i

    def stack(z):
        return jnp.concatenate([jnp.where(lo, z, 0.0), jnp.where(lo, 0.0, z)], axis=0).astype(BF16)

    npair = RW_DIM // LANES
    pairs = range(ns * npair)
    sls = [(rows[n // npair], slice(LANES * (n % npair), LANES * (n % npair + 1))) for n in pairs]
    ab = [abar[sl] for sl in sls]
    rb = [rbar[sl] for sl in sls]
    vv = [v[sl] for sl in sls]
    s0 = [state_ref[gp] for gp in pairs]
    abrb = [jnp.concatenate([ab[gp], rb[gp]], axis=0).astype(BF16) for gp in pairs]
    q4 = [_dot_nt(abrb[gp], jnp.concatenate([stack(bbar[sls[gp]]), stack(kbar[sls[gp]])], axis=0))
          for gp in pairs]
    ar = [_dot_nt(abrb[gp], s0[gp].astype(BF16)) for gp in pairs]
    a_ab = [jnp.where(m_strict, q[0:L, 0:LANES], 0.0) for q in q4]
    tinv = [jnp.where(eye, 1.0, a) for a in a_ab]
    pw = [_dot(a.astype(BF16), stack(a)) for a in a_ab]
    for _ in range(4):
        both = [_dot(jnp.concatenate([tinv[gp], pw[gp]], axis=0).astype(BF16), stack(pw[gp])) for gp in pairs]
        tinv = [tinv[gp] + both[gp][0:L] for gp in pairs]
        pw = [both[gp][L:] for gp in pairs]
    tinv = [tinv[gp] + _dot(tinv[gp].astype(BF16), stack(pw[gp])) for gp in pairs]
    v_st = [stack(z) for z in vv]
    rhs = [ar[gp][0:L] + _dot(jnp.where(m_strict, q4[gp][0:L, LANES:], 0.0).astype(BF16), v_st[gp])
           for gp in pairs]
    u = [_dot(tinv[gp].astype(BF16), stack(rhs[gp])) for gp in pairs]
    ys = [ar[gp][L:] + _dot(jnp.where(m_incl2, q4[gp][L:], 0.0).astype(BF16),
                            jnp.concatenate([stack(u[gp]), v_st[gp]], axis=0)) for gp in pairs]
    for gp in pairs:
        wl_gp = wl[gp // npair][:, sls[gp][1]]
        uv = jnp.concatenate([u[gp], vv[gp]], axis=0)
        bk = jnp.concatenate([bbar[sls[gp]], kbar[sls[gp]]], axis=0) * wl_gp
        s_add = _dot(uv.T.astype(BF16), bk.astype(BF16))
        state_ref[gp] = s0[gp] * wl_gp + jnp.where(same, s_add, 0.0)
    y = jnp.concatenate([jnp.concatenate(ys[s * npair:(s + 1) * npair], axis=1) for s in range(ns)], axis=0)

    inv_n = 1.0 / HEAD_DIM
    d = y - _dot_split(y, ones_bd) * inv_n
    var = _dot((d * d).astype(BF16), ones_bd) * inv_n
    yn = d * lax.rsqrt(var + LNX_EPS) * lnw_ref[...] + lnb_ref[...]
    bonus = _dot((r * kmod * rk_ref[...]).astype(BF16), ones_bd) * v
    y_ref[...] = ((yn + bonus) * g).astype(BF16).reshape(ns, L, RW_DIM)


def _rwkv(p_rw, batch, seq, mu, w0, a0, w_wa, g_up, k_k, k_a, r_k, lnx_w, lnx_b, ones_bd, ns=2):
    nc = seq // CHUNK
    vec = lambda: _const_spec((1, RW_DIM))
    idx = jnp.arange(ns * CHUNK)
    tri = ((idx[:, None] >= idx[None, :]) & (idx[:, None] // CHUNK == idx[None, :] // CHUNK)).astype(BF16)
    y = pl.pallas_call(
        _rwkv_body,
        grid=(batch // ns, nc),
        in_specs=[
            pl.BlockSpec((ns, CHUNK, RW_COLS), lambda b, c: (b, c, 0)),
            _const_spec((1, RW_COLS)), vec(), vec(),
            _const_spec((LANES, 2 * RW_DIM)), _const_spec((LANES, RW_DIM)),
            vec(), vec(), vec(), vec(), vec(),
            _const_spec((ns * CHUNK, ns * CHUNK)), _const_spec((RW_DIM, RW_DIM)),
        ],
        out_specs=pl.BlockSpec((ns, CHUNK, RW_DIM), lambda b, c: (b, c, 0)),
        out_shape=jax.ShapeDtypeStruct((batch, seq, RW_DIM), BF16),
        scratch_shapes=[
            pltpu.VMEM((8, RW_COLS), F32),
            pltpu.VMEM((ns * RW_DIM // LANES, LANES, LANES), F32),
        ],
        compiler_params=_params("arbitrary", "arbitrary"),
        name="rwkv",
    )(p_rw.reshape(batch, seq, RW_COLS), mu, w0, a0, w_wa, g_up, k_k, k_a, r_k, lnx_w, lnx_b, tri, ones_bd)
    return y.reshape(batch * seq, RW_DIM)


def _bucket_ranges():
    max_exact = NUM_BUCKETS // 2
    ratio_num, ratio_den = REL_MAX_DIST, max_exact
    bounds = []
    for b in range(NUM_BUCKETS):
        if b < max_exact:
            bounds.append((b, b + 1))
            continue
        def first(j):
            n = max_exact
            while (n ** (NUM_BUCKETS - max_exact)) * (ratio_den ** j) < (ratio_num ** j) * (max_exact ** (NUM_BUCKETS - max_exact)):
                n += 1
            return n
        lo = first(b - max_exact)
        hi = first(b + 1 - max_exact) if b < NUM_BUCKETS - 1 else 1 << 30
        bounds.append((lo, hi))
    return bounds


def _t5bias_body(tab_ref, o_ref, *, ranges):
    h = pl.program_id(0)
    blk = MOBA_BLOCK
    rel = lax.broadcasted_iota(jnp.int32, (blk, blk), 1) - lax.broadcasted_iota(jnp.int32, (blk, blk), 0)
    nblk = o_ref.shape[1] // blk
    for di in range(nblk):
        dist = rel + di * blk
        n = jnp.maximum(dist, 0)
        acc = jnp.zeros((blk, blk), F32)
        for b, (lo, hi) in enumerate(ranges):
            if hi <= di * blk - (blk - 1) or lo > di * blk + (blk - 1):
                continue
            acc = jnp.where((n >= lo) & (n < hi), tab_ref[h, b] * LOG2E, acc)
        o_ref[0, (nblk - 1 - di) * blk:(nblk - di) * blk, :] = jnp.where(dist >= 0, acc, NEG_INF)


def _t5bias(table, nblk):
    heads = table.shape[0]
    return pl.pallas_call(
        functools.partial(_t5bias_body, ranges=_bucket_ranges()),
        grid=(heads,),
        in_specs=[pl.BlockSpec(memory_space=pltpu.SMEM)],
        out_specs=pl.BlockSpec((1, nblk * MOBA_BLOCK, MOBA_BLOCK), lambda h: (h, 0, 0)),
        out_shape=jax.ShapeDtypeStruct((heads, nblk * MOBA_BLOCK, MOBA_BLOCK), F32),
        compiler_params=_params("arbitrary"),
        name="t5bias",
    )(table)


def _moba_body(k_ref, qt_ref, vt_ref, bias_ref, qg_ref, kg_ref, ones_ref, o_ref, kaug_ref, qaug_ref, vaug_ref,
               ot_ref):
    seq = k_ref.shape[0]
    blk = MOBA_BLOCK
    nb = seq // blk

    k2 = k_ref[...].astype(F32)
    ms = _dot((k2 * k2).astype(BF16), ones_ref[...]) * (1.0 / HEAD_DIM)
    for hh in range(2):
        vaug_ref[hh, 0:HEAD_DIM, :] = vt_ref[hh * HEAD_DIM:(hh + 1) * HEAD_DIM, :]
        vaug_ref[hh, HEAD_DIM:, :] = jnp.ones((vaug_ref.shape[1] - HEAD_DIM, seq), BF16)
    knf = k2 * lax.rsqrt(ms + NORM_EPS) * kg_ref[...]
    kmean = jnp.concatenate(
        [jnp.mean(knf[j * blk:(j + 1) * blk], axis=0, keepdims=True) for j in range(nb)], axis=0)
    klane = lax.broadcasted_iota(jnp.int32, (seq, LANES), 1)
    kblk = lax.broadcasted_iota(jnp.int32, (seq, LANES), 0) // blk
    kaug_ref[0] = jnp.where(klane < HEAD_DIM, knf, jnp.where(klane - HEAD_DIM == kblk, 1.0, 0.0)).astype(BF16)
    kaug_ref[1] = jnp.where(klane >= HEAD_DIM, knf, jnp.where(klane == kblk, 1.0, 0.0)).astype(BF16)

    q2 = qt_ref[...].astype(F32)

    def nrm(qh):
        return qh * lax.rsqrt(jnp.mean(qh * qh, axis=0, keepdims=True) + NORM_EPS)

    qn = jnp.concatenate([nrm(q2[0:HEAD_DIM]), nrm(q2[HEAD_DIM:])], axis=0) * qg_ref[...] \
        * (HEAD_DIM ** -0.5 * LOG2E)
    qnb = qn.astype(BF16)

    km_hi = kmean.astype(BF16)
    km_lo = (kmean - km_hi.astype(F32)).astype(BF16)
    lane = lax.broadcasted_iota(jnp.int32, (nb, LANES), 1)
    rowi = lax.broadcasted_iota(jnp.int32, (nb, seq), 0)
    qblk = lax.broadcasted_iota(jnp.int32, (nb, seq), 1) // blk
    past = rowi < qblk
    qblk1 = qblk[0:1]
    zero_b = jnp.zeros_like(km_hi)
    pad = jnp.zeros((HEAD_DIM - nb, seq), F32)

    for hh in range(2):
        hm = (lane < HEAD_DIM) if hh == 0 else (lane >= HEAD_DIM)
        gate = _dot(jnp.where(hm, km_hi, zero_b), qnb) + _dot(jnp.where(hm, km_lo, zero_b), qnb)
        rows = []
        for j in range(nb):
            gj = gate[j:j + 1, :]
            beats = (gate > gj) | ((gate == gj) & (rowi < j))
            cnt = jnp.sum(jnp.where(past & beats, 1.0, 0.0), axis=0, keepdims=True)
            sel = ((qblk1 > j) & (cnt < MOBA_TOPK)) | (qblk1 == j)
            rows.append(jnp.where(sel, 0.0, NEG_INF))
        selb = jnp.concatenate(rows, axis=0)
        if hh == 0:
            qaug_ref[hh] = jnp.concatenate([qn[0:HEAD_DIM], selb, pad], axis=0).astype(BF16)
        else:
            qaug_ref[hh] = jnp.concatenate([selb, pad, qn[HEAD_DIM:]], axis=0).astype(BF16)

    tiles = [(hh, i) for i in range(nb) for hh in range(2)]

    def scores(hh, i):
        nk = (i + 1) * blk
        return _dot(kaug_ref[hh, 0:nk, :], qaug_ref[hh, :, i * blk:(i + 1) * blk]) \
            + bias_ref[hh, (nb - 1 - i) * blk:, :]

    s_next = scores(*tiles[0])
    for n, (hh, i) in enumerate(tiles):
        s = s_next
        if n + 1 < len(tiles):
            s_next = scores(*tiles[n + 1])
        hrows = slice(hh * HEAD_DIM, (hh + 1) * HEAD_DIM)
        p = jnp.exp2(s - jnp.max(s, axis=0, keepdims=True)).astype(BF16)
        acc = _dot(vaug_ref[hh, :, 0:(i + 1) * blk], p)
        ot_ref[hrows, i * blk:(i + 1) * blk] = acc[0:HEAD_DIM] / acc[HEAD_DIM:HEAD_DIM + 1]

    for i in range(nb):
        o_ref[i * blk:(i + 1) * blk, :] = ot_ref[:, i * blk:(i + 1) * blk].T.astype(BF16)


def _moba(k_att, q_t, v_t, bias_t, qg, kg, ones_bd, batch, seq):
    n = batch * seq
    npair = ATT_DIM // LANES
    nb = seq // MOBA_BLOCK
    return pl.pallas_call(
        _moba_body,
        grid=(npair, batch),
        in_specs=[
            pl.BlockSpec((seq, LANES), lambda g, b: (b, g)),
            pl.BlockSpec((LANES, seq), lambda g, b: (g, b)),
            pl.BlockSpec((LANES, seq), lambda g, b: (g, b)),
            pl.BlockSpec((2, seq, MOBA_BLOCK), lambda g, b: (g, 0, 0)),
            _const_spec((LANES, 1)), _const_spec((1, LANES)), _const_spec((LANES, LANES)),
        ],
        out_specs=pl.BlockSpec((seq, LANES), lambda g, b: (b, g)),
        out_shape=jax.ShapeDtypeStruct((n, ATT_DIM), BF16),
        scratch_shapes=[
            pltpu.VMEM((2, seq, LANES), BF16),
            pltpu.VMEM((2, LANES, seq), BF16),
            pltpu.VMEM((2, HEAD_DIM + 16, seq), BF16),
            pltpu.VMEM((LANES, seq), F32),
        ],
        compiler_params=_params("arbitrary", "arbitrary"),
        name="moba",
    )(k_att, q_t, v_t, bias_t, qg, kg, ones_bd)


def _outproj_body(x_ref, yr_ref, ya_ref, wr_ref, wa_ref, o_ref):
    o_ref[...] = x_ref[...] + _dot(yr_ref[...], wr_ref[...]) + _dot(ya_ref[...], wa_ref[...])


def _outproj(x2, y_rw, y_att, w_r, w_a, tm=1024):
    n = x2.shape[0]
    return pl.pallas_call(
        _outproj_body,
        grid=(n // tm,),
        in_specs=[
            pl.BlockSpec((tm, D_MODEL), lambda i: (i, 0)),
            pl.BlockSpec((tm, RW_DIM), lambda i: (i, 0)),
            pl.BlockSpec((tm, ATT_DIM), lambda i: (i, 0)),
            _const_spec((RW_DIM, D_MODEL)), _const_spec((ATT_DIM, D_MODEL)),
        ],
        out_specs=pl.BlockSpec((tm, D_MODEL), lambda i: (i, 0)),
        out_shape=jax.ShapeDtypeStruct((n, D_MODEL), F32),
        compiler_params=_params("arbitrary"),
        name="outproj",
    )(x2, y_rw, y_att, w_r, w_a)


def _causal_conv3(h, carry_ref, cols, w):
    tm = h.shape[0]
    c = carry_ref[:, cols]
    row = lax.broadcasted_iota(jnp.int32, c.shape, 0)
    r1 = pltpu.roll(h, 1, axis=0)
    r2 = pltpu.roll(h, 2, axis=0)
    h1 = jnp.concatenate([jnp.where(row == 0, c[7:8], r1[0:8]), r1[8:]], axis=0)
    h2 = jnp.concatenate([jnp.where(row == 0, c[6:7], jnp.where(row == 1, c[7:8], r2[0:8])), r2[8:]], axis=0)
    carry_ref[:, cols] = h[tm - 8:tm]
    return w[2:3] * h + w[1:2] * h1 + w[0:1] * h2


def _ffn_body(x_ref, g_ref, up_ref, cw_ref, cb_ref, down_ref, o_ref, xn_ref, z_ref, carry_ref, *, tc):
    @pl.when(pl.program_id(1) == 0)
    def _():
        carry_ref[...] = jnp.zeros_like(carry_ref)

    xn_ref[...] = _rms_norm_rows(x_ref[...], g_ref[...]).astype(BF16)

    def branch(c0):
        cols = slice(c0, c0 + tc)
        h = _dot(xn_ref[...], up_ref[:, cols])
        return _causal_conv3(h, carry_ref, cols, cw_ref[:, cols]) + cb_ref[:, cols]

    for c0 in range(0, D_FF, tc):
        za = branch(c0)
        zu = branch(D_FF + c0)
        z_ref[:, c0:c0 + tc] = (za * jax.nn.sigmoid(za) * zu).astype(BF16)
    o_ref[...] = x_ref[...] + _dot(z_ref[...], down_ref[...])


def _ffn(x2, batch, seq, g, up, conv_w, conv_b, down, tm=1024, tc=256):
    n = batch * seq
    nt = seq // tm
    return pl.pallas_call(
        functools.partial(_ffn_body, tc=tc),
        grid=(batch, nt),
        in_specs=[
            pl.BlockSpec((tm, D_MODEL), lambda b, t: (b * nt + t, 0)),
            _const_spec((1, D_MODEL)),
            _const_spec((D_MODEL, 2 * D_FF)),
            _const_spec((3, 2 * D_FF)),
            _const_spec((1, 2 * D_FF)),
            _const_spec((D_FF, D_MODEL)),
        ],
        out_specs=pl.BlockSpec((tm, D_MODEL), lambda b, t: (b * nt + t, 0)),
        out_shape=jax.ShapeDtypeStruct((n, D_MODEL), F32),
        scratch_shapes=[pltpu.VMEM((tm, D_MODEL), BF16), pltpu.VMEM((tm, D_FF), BF16),
                        pltpu.VMEM((8, 2 * D_FF), F32)],
        compiler_params=_params("arbitrary", "arbitrary"),
        name="ffn",
    )(x2, g, up, conv_w, conv_b, down)


def _sconv_body(x_ref, g_ref, win_ref, cw_ref, wout_ref, o_ref, xn_ref, z_ref, carry_ref, *, tc):
    @pl.when(pl.program_id(1) == 0)
    def _():
        carry_ref[...] = jnp.zeros_like(carry_ref)

    xn_ref[...] = _rms_norm_rows(x_ref[...], g_ref[...]).astype(BF16)

    for c0 in range(0, D_MODEL, tc):
        cols = slice(c0, c0 + tc)
        xn = xn_ref[...]
        b_gate = _dot(xn, win_ref[:, cols])
        c_gate = _dot(xn, win_ref[:, D_MODEL + c0:D_MODEL + c0 + tc])
        hid = _dot(xn, win_ref[:, 2 * D_MODEL + c0:2 * D_MODEL + c0 + tc])
        conv = _causal_conv3(c_gate * hid, carry_ref, cols, cw_ref[:, cols])
        z_ref[:, cols] = (b_gate * conv).astype(BF16)
    o_ref[...] = x_ref[...] + _dot(z_ref[...], wout_ref[...])


def _sconv(x2, batch, seq, g, w_in, conv_w, w_out, tm=1024, tc=256):
    n = batch * seq
    nt = seq // tm
    return pl.pallas_call(
        functools.partial(_sconv_body, tc=tc),
        grid=(batch, nt),
        in_specs=[
            pl.BlockSpec((tm, D_MODEL), lambda b, t: (b * nt + t, 0)),
            _const_spec((1, D_MODEL)),
            _const_spec((D_MODEL, 3 * D_MODEL)),
            _const_spec((3, D_MODEL)),
            _const_spec((D_MODEL, D_MODEL)),
        ],
        out_specs=pl.BlockSpec((tm, D_MODEL), lambda b, t: (b * nt + t, 0)),
        out_shape=jax.ShapeDtypeStruct((n, D_MODEL), F32),
        scratch_shapes=[pltpu.VMEM((tm, D_MODEL), BF16), pltpu.VMEM((tm, D_MODEL), BF16),
                        pltpu.VMEM((8, D_MODEL), F32)],
        compiler_params=_params("arbitrary", "arbitrary"),
        name="sconv",
    )(x2, g, w_in, conv_w, w_out)


def _block_diag_ones(n):
    i = jnp.arange(n) // HEAD_DIM
    return (i[:, None] == i[None, :]).astype(BF16)


def _rwkv_moba_mixer(x2, batch, seq, rel_bias, mix_norm, w_in, shift_mu, w0, w_lora_up, a0, a_lora_up, g_lora_up,
                     k_k, k_a, r_k, lnx_w, lnx_b, q_norm, k_norm, w_out):
    row = lambda z: z.reshape(1, -1).astype(F32)
    w_in = w_in.astype(BF16)
    q0 = RW_COLS
    w_rw = w_in[:, :q0]
    w_qt = w_in[:, q0:q0 + ATT_DIM].T
    w_k = w_in[:, q0 + ATT_DIM:q0 + 2 * ATT_DIM]
    w_vt = w_in[:, q0 + 2 * ATT_DIM:].T
    p_rw, k_att, q_t, v_t = _inproj0(x2, row(mix_norm), w_rw, w_k, w_qt, w_vt)

    zeros = jnp.zeros_like(w_lora_up)
    w_wa = jnp.concatenate([jnp.concatenate([w_lora_up, zeros], axis=1),
                            jnp.concatenate([zeros, a_lora_up], axis=1)], axis=0).astype(BF16)
    y_rw = _rwkv(p_rw, batch, seq, row(shift_mu), row(w0), row(a0), w_wa, g_lora_up.astype(BF16),
                 row(k_k), row(k_a), row(r_k), row(lnx_w), row(lnx_b), _block_diag_ones(RW_DIM),
                 ns=4 if batch % 4 == 0 else 1)

    bias_t = _t5bias(rel_bias.T.astype(F32), seq // MOBA_BLOCK)
    qg = jnp.tile(q_norm.astype(F32), 2).reshape(LANES, 1)
    kg = jnp.tile(k_norm.astype(F32), 2).reshape(1, LANES)
    y_att = _moba(k_att, q_t, v_t, bias_t, qg, kg, _block_diag_ones(LANES), batch, seq)

    w_out = w_out.astype(BF16)
    return _outproj(x2, y_rw, y_att, w_out[:RW_DIM], w_out[RW_DIM:])


def kernel(x, rel_bias, l0_mix_norm, l0_w_in, l0_shift_mu, l0_w0, l0_w_lora_up, l0_a0, l0_a_lora_up, l0_g_lora_up, l0_k_k, l0_k_a, l0_r_k, l0_lnx_w, l0_lnx_b, l0_q_norm, l0_k_norm, l0_w_out, l0_ffn_norm, l0_ffn_up, l0_ffn_conv_w, l0_ffn_conv_b, l0_ffn_down, l1_mix_norm, l1_w_in, l1_conv_w, l1_w_out, l1_ffn_norm, l1_ffn_up, l1_ffn_conv_w, l1_ffn_conv_b, l1_ffn_down):
    batch, seq, d = x.shape
    row = lambda z: z.reshape(1, -1).astype(F32)
    x2 = x.reshape(batch * seq, d)
    x2 = _rwkv_moba_mixer(x2, batch, seq, rel_bias, l0_mix_norm, l0_w_in, l0_shift_mu, l0_w0, l0_w_lora_up, l0_a0,
                          l0_a_lora_up, l0_g_lora_up, l0_k_k, l0_k_a, l0_r_k, l0_lnx_w, l0_lnx_b, l0_q_norm,
                          l0_k_norm, l0_w_out)
    x2 = _ffn(x2, batch, seq, row(l0_ffn_norm), l0_ffn_up.astype(BF16), l0_ffn_conv_w.astype(F32),
              row(l0_ffn_conv_b), l0_ffn_down.astype(BF16))
    x2 = _sconv(x2, batch, seq, row(l1_mix_norm), l1_w_in.astype(BF16), l1_conv_w.astype(F32),
                l1_w_out.astype(BF16))
    x2 = _ffn(x2, batch, seq, row(l1_ffn_norm), l1_ffn_up.astype(BF16), l1_ffn_conv_w.astype(F32),
              row(l1_ffn_conv_b), l1_ffn_down.astype(BF16))
    return x2.reshape(batch, seq, d)
```

```python
import functools
import math

import jax
import jax.numpy as jnp
from jax import lax
from jax.experimental import pallas as pl
from jax.experimental.pallas import tpu as pltpu

F32 = jnp.float32
BF16 = jnp.bfloat16

D_MODEL = 1024
HEAD_DIM = 64
RW_DIM = 512
RW_COLS = 3 * RW_DIM + 64 + 64 + 128
ATT_DIM = 512
MOBA_BLOCK = 256
MOBA_TOPK = 3
NUM_BUCKETS = 32
REL_MAX_DIST = 1024
D_FF = 2816
NORM_EPS = 1e-6
LNX_EPS = 64e-5
NEG_INF = -1e30
LOG2E = math.log2(math.e)
MAX_BRACKET = 160.0
BOUND_SLACK = 1.02

CHUNK = 64
LANES = 128
VMEM_LIMIT = 52 * 1024 * 1024


def _dot(a, b):
    return jnp.dot(a, b, preferred_element_type=F32)


def _dot_nt(a, b):
    return lax.dot_general(a, b, (((1,), (1,)), ((), ())), preferred_element_type=F32)


def _dot_split(x, w):
    hi = x.astype(BF16)
    lo = (x - hi.astype(F32)).astype(BF16)
    return _dot(hi, w) + _dot(lo, w)


def _rms_norm_rows(x, g):
    return x * lax.rsqrt(jnp.mean(x * x, axis=-1, keepdims=True) + NORM_EPS) * g


def _const_spec(shape):
    return pl.BlockSpec(shape, lambda *_: (0,) * len(shape), pipeline_mode=pl.Buffered(1))


def _params(*sem):
    return pltpu.CompilerParams(dimension_semantics=sem, vmem_limit_bytes=VMEM_LIMIT)


def _inproj0_body(x_ref, g_ref, wrw_ref, wk_ref, wqt_ref, wvt_ref, prw_ref, k_ref, qt_ref, vt_ref):
    xn = _rms_norm_rows(x_ref[...], g_ref[...]).astype(BF16)
    prw_ref[...] = _dot(xn, wrw_ref[...])
    k_ref[...] = _dot(xn, wk_ref[...]).astype(BF16)
    qt_ref[...] = _dot_nt(wqt_ref[...], xn).astype(BF16)
    vt_ref[...] = _dot_nt(wvt_ref[...], xn).astype(BF16)


def _inproj0(x2, g, w_rw, w_k, w_qt, w_vt, tm=512):
    n = x2.shape[0]
    return pl.pallas_call(
        _inproj0_body,
        grid=(n // tm,),
        in_specs=[
            pl.BlockSpec((tm, D_MODEL), lambda i: (i, 0)),
            _const_spec((1, D_MODEL)),
            _const_spec((D_MODEL, RW_COLS)),
            _const_spec((D_MODEL, ATT_DIM)),
            _const_spec((ATT_DIM, D_MODEL)),
            _const_spec((ATT_DIM, D_MODEL)),
        ],
        out_specs=[
            pl.BlockSpec((tm, RW_COLS), lambda i: (i, 0)),
            pl.BlockSpec((tm, ATT_DIM), lambda i: (i, 0)),
            pl.BlockSpec((ATT_DIM, tm), lambda i: (0, i)),
            pl.BlockSpec((ATT_DIM, tm), lambda i: (0, i)),
        ],
        out_shape=[
            jax.ShapeDtypeStruct((n, RW_COLS), F32),
            jax.ShapeDtypeStruct((n, ATT_DIM), BF16),
            jax.ShapeDtypeStruct((ATT_DIM, n), BF16),
            jax.ShapeDtypeStruct((ATT_DIM, n), BF16),
        ],
        compiler_params=_params("arbitrary"),
        name="inproj0",
    )(x2, g, w_rw, w_k, w_qt, w_vt)


def _rwkv_body(p_ref, mu_ref, w0_ref, a0_ref, wwa_ref, gup_ref, kk_ref, ka_ref, rk_ref, lnw_ref, lnb_ref,
               tri_ref, ones_ref, y_ref, carry_ref, state_ref):
    L = CHUNK
    ns = p_ref.shape[0]
    rows = [slice(s * L, (s + 1) * L) for s in range(ns)]

    @pl.when(pl.program_id(1) == 0)
    def _():
        carry_ref[...] = jnp.zeros_like(carry_ref)
        state_ref[...] = jnp.zeros_like(state_ref)

    p = p_ref[...].reshape(ns * L, RW_COLS)
    row = lax.broadcasted_iota(jnp.int32, p.shape, 0)
    prev = pltpu.roll(p, 1, axis=0)
    for s in range(ns):
        prev = jnp.where(row == s * L, carry_ref[s:s + 1, :], prev)
    for s in range(ns):
        carry_ref[s:s + 1, :] = p[(s + 1) * L - 1:(s + 1) * L, :]
    ps = p + (prev - p) * mu_ref[...]
    r = ps[:, 0:RW_DIM]
    k = ps[:, RW_DIM:2 * RW_DIM]
    v = ps[:, 2 * RW_DIM:3 * RW_DIM]
    slab = ps[:, 3 * RW_DIM:3 * RW_DIM + LANES]
    dg = ps[:, 3 * RW_DIM + LANES:RW_COLS]
    lo = lax.broadcasted_iota(jnp.int32, (L, LANES), 1) < HEAD_DIM
    lo_all = lax.broadcasted_iota(jnp.int32, slab.shape, 1) < HEAD_DIM

    wa = _dot(jnp.where(lo_all, jnp.tanh(slab), slab).astype(BF16), wwa_ref[...])
    ld = -math.exp(-0.5) * jax.nn.sigmoid(w0_ref[...] + wa[:, :RW_DIM])
    a = jax.nn.sigmoid(a0_ref[...] + wa[:, RW_DIM:])
    g = _dot(jax.nn.sigmoid(dg).astype(BF16), gup_ref[...])
    ones_bd = ones_ref[...]
    kk = k * kk_ref[...]
    kk = kk * lax.rsqrt(jnp.maximum(_dot((kk * kk).astype(BF16), ones_bd), 1e-24))
    kmod = k * (1.0 + (a - 1.0) * ka_ref[...])

    tri = tri_ref[...]
    ld_hi = ld.astype(BF16)
    rem = ld - ld_hi.astype(F32)
    ld_mid = rem.astype(BF16)
    ld_lo = (rem - ld_mid.astype(F32)).astype(BF16)
    cs = _dot(tri, ld_hi) + _dot(tri, ld_mid) + _dot(tri, ld_lo)
    e_c = jnp.exp(cs)
    e_nc = jnp.exp(-cs)
    abar = -kk * jnp.exp(cs - ld)
    rbar = r * e_c
    bbar = kk * a * e_nc
    kbar = kmod * e_nc
    wl = [e_c[(s + 1) * L - 1:(s + 1) * L, :] for s in range(ns)]

    ri = lax.broadcasted_iota(jnp.int32, (LANES, LANES), 0)
    ci = lax.broadcasted_iota(jnp.int32, (LANES, LANES), 1)
    same = (ri >= HEAD_DIM) == (ci >= HEAD_DIM)
    t_i = lax.broadcasted_iota(jnp.int32, (L, LANES), 0)
    s_i = lax.broadcasted_iota(jnp.int32, (L, LANES), 1) & (L - 1)
    m_strict = s_i < t_i
    m_incl = s_i <= t_i
    m_incl2 = jnp.concatenate([m_incl, m_incl], axis=1)
    eye = s_i == t_i

    def stack(z):
        return jnp.concatenate([jnp.where(lo, z, 0.0), jnp.where(lo, 0.0, z)], axis=0).astype(BF16)

    npair = RW_DIM // LANES
    pairs = range(ns * npair)
    sls = [(rows[n // npair], slice(LANES * (n % npair), LANES * (n % npair + 1))) for n in pairs]
    ab = [abar[sl] for sl in sls]
    rb = [rbar[sl] for sl in sls]
    vv = [v[sl] for sl in sls]
    s0 = [state_ref[gp] for gp in pairs]
    abrb = [jnp.concatenate([ab[gp], rb[gp]], axis=0).astype(BF16) for gp in pairs]
    q4 = [_dot_nt(abrb[gp], jnp.concatenate([stack(bbar[sls[gp]]), stack(kbar[sls[gp]])], axis=0))
          for gp in pairs]
    ar = [_dot_nt(abrb[gp], s0[gp].astype(BF16)) for gp in pairs]
    a_ab = [jnp.where(m_strict, q[0:L, 0:LANES], 0.0) for q in q4]
    tinv = [jnp.where(eye, 1.0, a) for a in a_ab]
    pw = [_dot(a.astype(BF16), stack(a)) for a in a_ab]
    for _ in range(4):
        both = [_dot(jnp.concatenate([tinv[gp], pw[gp]], axis=0).astype(BF16), stack(pw[gp])) for gp in pairs]
        tinv = [tinv[gp] + both[gp][0:L] for gp in pairs]
        pw = [both[gp][L:] for gp in pairs]
    tinv = [tinv[gp] + _dot(tinv[gp].astype(BF16), stack(pw[gp])) for gp in pairs]
    v_st = [stack(z) for z in vv]
    rhs = [ar[gp][0:L] + _dot(jnp.where(m_strict, q4[gp][0:L, LANES:], 0.0).astype(BF16), v_st[gp])
           for gp in pairs]
    u = [_dot(tinv[gp].astype(BF16), stack(rhs[gp])) for gp in pairs]
    ys = [ar[gp][L:] + _dot(jnp.where(m_incl2, q4[gp][L:], 0.0).astype(BF16),
                            jnp.concatenate([stack(u[gp]), v_st[gp]], axis=0)) for gp in pairs]
    for gp in pairs:
        wl_gp = wl[gp // npair][:, sls[gp][1]]
        uv = jnp.concatenate([u[gp], vv[gp]], axis=0)
        bk = jnp.concatenate([bbar[sls[gp]], kbar[sls[gp]]], axis=0) * wl_gp
        s_add = _dot(uv.T.astype(BF16), bk.astype(BF16))
        state_ref[gp] = s0[gp] * wl_gp + jnp.where(same, s_add, 0.0)
    y = jnp.concatenate([jnp.concatenate(ys[s * npair:(s + 1) * npair], axis=1) for s in range(ns)], axis=0)

    inv_n = 1.0 / HEAD_DIM
    d = y - _dot_split(y, ones_bd) * inv_n
    var = _dot((d * d).astype(BF16), ones_bd) * inv_n
    yn = d * lax.rsqrt(var + LNX_EPS) * lnw_ref[...] + lnb_ref[...]
    bonus = _dot((r * kmod * rk_ref[...]).astype(BF16), ones_bd) * v
    y_ref[...] = ((yn + bonus) * g).astype(BF16).reshape(ns, L, RW_DIM)


def _rwkv(p_rw, batch, seq, mu, w0, a0, w_wa, g_up, k_k, k_a, r_k, lnx_w, lnx_b, ones_bd, ns=2):
    nc = seq // CHUNK
    vec = lambda: _const_spec((1, RW_DIM))
    idx = jnp.arange(ns * CHUNK)
    tri = ((idx[:, None] >= idx[None, :]) & (idx[:, None] // CHUNK == idx[None, :] // CHUNK)).astype(BF16)
    y = pl.pallas_call(
        _rwkv_body,
        grid=(batch // ns, nc),
        in_specs=[
            pl.BlockSpec((ns, CHUNK, RW_COLS), lambda b, c: (b, c, 0)),
            _const_spec((1, RW_COLS)), vec(), vec(),
            _const_spec((LANES, 2 * RW_DIM)), _const_spec((LANES, RW_DIM)),
            vec(), vec(), vec(), vec(), vec(),
            _const_spec((ns * CHUNK, ns * CHUNK)), _const_spec((RW_DIM, RW_DIM)),
        ],
        out_specs=pl.BlockSpec((ns, CHUNK, RW_DIM), lambda b, c: (b, c, 0)),
        out_shape=jax.ShapeDtypeStruct((batch, seq, RW_DIM), BF16),
        scratch_shapes=[
            pltpu.VMEM((8, RW_COLS), F32),
            pltpu.VMEM((ns * RW_DIM // LANES, LANES, LANES), F32),
        ],
        compiler_params=_params("arbitrary", "arbitrary"),
        name="rwkv",
    )(p_rw.reshape(batch, seq, RW_COLS), mu, w0, a0, w_wa, g_up, k_k, k_a, r_k, lnx_w, lnx_b, tri, ones_bd)
    return y.reshape(batch * seq, RW_DIM)


def _bucket_ranges():
    max_exact = NUM_BUCKETS // 2
    ratio_num, ratio_den = REL_MAX_DIST, max_exact
    bounds = []
    for b in range(NUM_BUCKETS):
        if b < max_exact:
            bounds.append((b, b + 1))
            continue
        def first(j):
            n = max_exact
            while (n ** (NUM_BUCKETS - max_exact)) * (ratio_den ** j) < (ratio_num ** j) * (max_exact ** (NUM_BUCKETS - max_exact)):
                n += 1
            return n
        lo = first(b - max_exact)
        hi = first(b + 1 - max_exact) if b < NUM_BUCKETS - 1 else 1 << 30
        bounds.append((lo, hi))
    return bounds


def _t5bias_body(tab_ref, o_ref, *, ranges):
    h = pl.program_id(0)
    blk = MOBA_BLOCK
    rel = lax.broadcasted_iota(jnp.int32, (blk, blk), 1) - lax.broadcasted_iota(jnp.int32, (blk, blk), 0)
    nblk = o_ref.shape[1] // blk
    for di in range(nblk):
        dist = rel + di * blk
        n = jnp.maximum(dist, 0)
        acc = jnp.zeros((blk, blk), F32)
        for b, (lo, hi) in enumerate(ranges):
            if hi <= di * blk - (blk - 1) or lo > di * blk + (blk - 1):
                continue
            acc = jnp.where((n >= lo) & (n < hi), tab_ref[h, b] * LOG2E, acc)
        o_ref[0, (nblk - 1 - di) * blk:(nblk - di) * blk, :] = jnp.where(dist >= 0, acc, NEG_INF)


def _t5bias(table, nblk):
    heads = table.shape[0]
    return pl.pallas_call(
        functools.partial(_t5bias_body, ranges=_bucket_ranges()),
        grid=(heads,),
        in_specs=[pl.BlockSpec(memory_space=pltpu.SMEM)],
        out_specs=pl.BlockSpec((1, nblk * MOBA_BLOCK, MOBA_BLOCK), lambda h: (h, 0, 0)),
        out_shape=jax.ShapeDtypeStruct((heads, nblk * MOBA_BLOCK, MOBA_BLOCK), F32),
        compiler_params=_params("arbitrary"),
        name="t5bias",
    )(table)


def _moba_body(k_ref, qt_ref, vt_ref, bias_ref, qg_ref, kg_ref, ones_ref, o_ref, kaug_ref, qaug_ref, vaug_ref,
               ot_ref):
    seq = k_ref.shape[0]
    blk = MOBA_BLOCK
    nb = seq // blk

    k2 = k_ref[...].astype(F32)
    ms = _dot((k2 * k2).astype(BF16), ones_ref[...]) * (1.0 / HEAD_DIM)
    for hh in range(2):
        vaug_ref[hh, 0:HEAD_DIM, :] = vt_ref[hh * HEAD_DIM:(hh + 1) * HEAD_DIM, :]
        vaug_ref[hh, HEAD_DIM:, :] = jnp.ones((vaug_ref.shape[1] - HEAD_DIM, seq), BF16)
    knf = k2 * lax.rsqrt(ms + NORM_EPS) * kg_ref[...]
    kmean = jnp.concatenate(
        [jnp.mean(knf[j * blk:(j + 1) * blk], axis=0, keepdims=True) for j in range(nb)], axis=0)
    klane = lax.broadcasted_iota(jnp.int32, (seq, LANES), 1)
    kblk = lax.broadcasted_iota(jnp.int32, (seq, LANES), 0) // blk
    kaug_ref[0] = jnp.where(klane < HEAD_DIM, knf, jnp.where(klane - HEAD_DIM == kblk, 1.0, 0.0)).astype(BF16)
    kaug_ref[1] = jnp.where(klane >= HEAD_DIM, knf, jnp.where(klane == kblk, 1.0, 0.0)).astype(BF16)

    q2 = qt_ref[...].astype(F32)

    def nrm(qh):
        return qh * lax.rsqrt(jnp.mean(qh * qh, axis=0, keepdims=True) + NORM_EPS)

    qn = jnp.concatenate([nrm(q2[0:HEAD_DIM]), nrm(q2[HEAD_DIM:])], axis=0) * qg_ref[...] \
        * (HEAD_DIM ** -0.5 * LOG2E)
    qnb = qn.astype(BF16)

    km_hi = kmean.astype(BF16)
    km_lo = (kmean - km_hi.astype(F32)).astype(BF16)
    lane = lax.broadcasted_iota(jnp.int32, (nb, LANES), 1)
    rowi = lax.broadcasted_iota(jnp.int32, (nb, seq), 0)
    qblk = lax.broadcasted_iota(jnp.int32, (nb, seq), 1) // blk
    past = rowi < qblk
    qblk1 = qblk[0:1]
    zero_b = jnp.zeros_like(km_hi)
    pad = jnp.zeros((HEAD_DIM - nb, seq), F32)

    ksq = _dot((knf * knf).astype(BF16), ones_ref[...])
    kmax2 = jnp.max(ksq, axis=0, keepdims=True)
    qk_own = qn * knf.T
    gaps = []

    for hh in range(2):
        hs = slice(hh * HEAD_DIM, (hh + 1) * HEAD_DIM)
        kmax = jnp.sqrt(jnp.max(kmax2[:, hs], axis=1, keepdims=True))
        bmax = jnp.max(jnp.max(bias_ref[hh], axis=0, keepdims=True), axis=1, keepdims=True)
        own_bias = bias_ref[hh, (nb - 1) * blk:(nb - 1) * blk + 1, 0:1]
        qlen = jnp.sqrt(jnp.sum(qn[hs] * qn[hs], axis=0, keepdims=True))
        ub = qlen * kmax * BOUND_SLACK + bmax
        lb = jnp.sum(qk_own[hs], axis=0, keepdims=True) + own_bias
        gaps.append(ub - lb)
        shift = 0.5 * (ub + lb)

        hm = (lane < HEAD_DIM) if hh == 0 else (lane >= HEAD_DIM)
        gate = _dot(jnp.where(hm, km_hi, zero_b), qnb) + _dot(jnp.where(hm, km_lo, zero_b), qnb)
        rows = []
        for j in range(nb):
            gj = gate[j:j + 1, :]
            beats = (gate > gj) | ((gate == gj) & (rowi < j))
            cnt = jnp.sum(jnp.where(past & beats, 1.0, 0.0), axis=0, keepdims=True)
            sel = ((qblk1 > j) & (cnt < MOBA_TOPK)) | (qblk1 == j)
            rows.append(jnp.where(sel, 0.0, NEG_INF))
        selb = jnp.concatenate(rows, axis=0)
        for variant, sb in ((0, selb), (2, selb - shift)):
            if hh == 0:
                qaug_ref[variant + hh] = jnp.concatenate([qn[0:HEAD_DIM], sb, pad], axis=0).astype(BF16)
            else:
                qaug_ref[variant + hh] = jnp.concatenate([sb, pad, qn[HEAD_DIM:]], axis=0).astype(BF16)

    tiles = [(hh, i) for i in range(nb) for hh in range(2)]

    def scores(variant, hh, i):
        nk = (i + 1) * blk
        return _dot(kaug_ref[hh, 0:nk, :], qaug_ref[variant + hh, :, i * blk:(i + 1) * blk]) \
            + bias_ref[hh, (nb - 1 - i) * blk:, :]

    def attend(variant, probs):
        s_next = scores(variant, *tiles[0])
        for n, (hh, i) in enumerate(tiles):
            s = s_next
            if n + 1 < len(tiles):
                s_next = scores(variant, *tiles[n + 1])
            acc = _dot(vaug_ref[hh, :, 0:(i + 1) * blk], probs(s))
            ot_ref[hh * HEAD_DIM:(hh + 1) * HEAD_DIM, i * blk:(i + 1) * blk] = \
                acc[0:HEAD_DIM] / acc[HEAD_DIM:HEAD_DIM + 1]

    narrow = jnp.max(jnp.maximum(gaps[0], gaps[1])) <= MAX_BRACKET

    @pl.when(narrow)
    def _():
        attend(2, lambda s: jnp.exp2(s).astype(BF16))

    @pl.when(jnp.logical_not(narrow))
    def _():
        attend(0, lambda s: jnp.exp2(s - jnp.max(s, axis=0, keepdims=True)).astype(BF16))

    for i in range(nb):
        o_ref[i * blk:(i + 1) * blk, :] = ot_ref[:, i * blk:(i + 1) * blk].T.astype(BF16)


def _moba(k_att, q_t, v_t, bias_t, qg, kg, ones_bd, batch, seq):
    n = batch * seq
    npair = ATT_DIM // LANES
    nb = seq // MOBA_BLOCK
    return pl.pallas_call(
        _moba_body,
        grid=(npair, batch),
        in_specs=[
            pl.BlockSpec((seq, LANES), lambda g, b: (b, g)),
            pl.BlockSpec((LANES, seq), lambda g, b: (g, b)),
            pl.BlockSpec((LANES, seq), lambda g, b: (g, b)),
            pl.BlockSpec((2, seq, MOBA_BLOCK), lambda g, b: (g, 0, 0)),
            _const_spec((LANES, 1)), _const_spec((1, LANES)), _const_spec((LANES, LANES)),
        ],
        out_specs=pl.BlockSpec((seq, LANES), lambda g, b: (b, g)),
        out_shape=jax.ShapeDtypeStruct((n, ATT_DIM), BF16),
        scratch_shapes=[
            pltpu.VMEM((2, seq, LANES), BF16),
            pltpu.VMEM((4, LANES, seq), BF16),
            pltpu.VMEM((2, HEAD_DIM + 16, seq), BF16),
            pltpu.VMEM((LANES, seq), F32),
        ],
        compiler_params=_params("arbitrary", "arbitrary"),
        name="moba",
    )(k_att, q_t, v_t, bias_t, qg, kg, ones_bd)


def _outproj_body(x_ref, yr_ref, ya_ref, wr_ref, wa_ref, o_ref):
    o_ref[...] = x_ref[...] + _dot(yr_ref[...], wr_ref[...]) + _dot(ya_ref[...], wa_ref[...])


def _outproj(x2, y_rw, y_att, w_r, w_a, tm=1024):
    n = x2.shape[0]
    return pl.pallas_call(
        _outproj_body,
        grid=(n // tm,),
        in_specs=[
            pl.BlockSpec((tm, D_MODEL), lambda i: (i, 0)),
            pl.BlockSpec((tm, RW_DIM), lambda i: (i, 0)),
            pl.BlockSpec((tm, ATT_DIM), lambda i: (i, 0)),
            _const_spec((RW_DIM, D_MODEL)), _const_spec((ATT_DIM, D_MODEL)),
        ],
        out_specs=pl.BlockSpec((tm, D_MODEL), lambda i: (i, 0)),
        out_shape=jax.ShapeDtypeStruct((n, D_MODEL), F32),
        compiler_params=_params("arbitrary"),
        name="outproj",
    )(x2, y_rw, y_att, w_r, w_a)


def _causal_conv3(h, carry_ref, cols, w):
    tm = h.shape[0]
    c = carry_ref[:, cols]
    row = lax.broadcasted_iota(jnp.int32, c.shape, 0)
    r1 = pltpu.roll(h, 1, axis=0)
    r2 = pltpu.roll(h, 2, axis=0)
    h1 = jnp.concatenate([jnp.where(row == 0, c[7:8], r1[0:8]), r1[8:]], axis=0)
    h2 = jnp.concatenate([jnp.where(row == 0, c[6:7], jnp.where(row == 1, c[7:8], r2[0:8])), r2[8:]], axis=0)
    carry_ref[:, cols] = h[tm - 8:tm]
    return w[2:3] * h + w[1:2] * h1 + w[0:1] * h2


def _ffn_body(x_ref, g_ref, up_ref, cw_ref, cb_ref, down_ref, o_ref, xn_ref, z_ref, carry_ref, *, tc):
    @pl.when(pl.program_id(1) == 0)
    def _():
        carry_ref[...] = jnp.zeros_like(carry_ref)

    xn_ref[...] = _rms_norm_rows(x_ref[...], g_ref[...]).astype(BF16)

    def branch(c0):
        cols = slice(c0, c0 + tc)
        h = _dot(xn_ref[...], up_ref[:, cols])
        return _causal_conv3(h, carry_ref, cols, cw_ref[:, cols]) + cb_ref[:, cols]

    for c0 in range(0, D_FF, tc):
        za = branch(c0)
        zu = branch(D_FF + c0)
        z_ref[:, c0:c0 + tc] = (za * jax.nn.sigmoid(za) * zu).astype(BF16)
    o_ref[...] = x_ref[...] + _dot(z_ref[...], down_ref[...])


def _ffn(x2, batch, seq, g, up, conv_w, conv_b, down, tm=1024, tc=256):
    n = batch * seq
    nt = seq // tm
    return pl.pallas_call(
        functools.partial(_ffn_body, tc=tc),
        grid=(batch, nt),
        in_specs=[
            pl.BlockSpec((tm, D_MODEL), lambda b, t: (b * nt + t, 0)),
            _const_spec((1, D_MODEL)),
            _const_spec((D_MODEL, 2 * D_FF)),
            _const_spec((3, 2 * D_FF)),
            _const_spec((1, 2 * D_FF)),
            _const_spec((D_FF, D_MODEL)),
        ],
        out_specs=pl.BlockSpec((tm, D_MODEL), lambda b, t: (b * nt + t, 0)),
        out_shape=jax.ShapeDtypeStruct((n, D_MODEL), F32),
        scratch_shapes=[pltpu.VMEM((tm, D_MODEL), BF16), pltpu.VMEM((tm, D_FF), BF16),
                        pltpu.VMEM((8, 2 * D_FF), F32)],
        compiler_params=_params("arbitrary", "arbitrary"),
        name="ffn",
    )(x2, g, up, conv_w, conv_b, down)


def _sconv_body(x_ref, g_ref, win_ref, cw_ref, wout_ref, o_ref, xn_ref, z_ref, carry_ref, *, tc):
    @pl.when(pl.program_id(1) == 0)
    def _():
        carry_ref[...] = jnp.zeros_like(carry_ref)

    xn_ref[...] = _rms_norm_rows(x_ref[...], g_ref[...]).astype(BF16)

    for c0 in range(0, D_MODEL, tc):
        cols = slice(c0, c0 + tc)
        xn = xn_ref[...]
        b_gate = _dot(xn, win_ref[:, cols])
        c_gate = _dot(xn, win_ref[:, D_MODEL + c0:D_MODEL + c0 + tc])
        hid = _dot(xn, win_ref[:, 2 * D_MODEL + c0:2 * D_MODEL + c0 + tc])
        conv = _causal_conv3(c_gate * hid, carry_ref, cols, cw_ref[:, cols])
        z_ref[:, cols] = (b_gate * conv).astype(BF16)
    o_ref[...] = x_ref[...] + _dot(z_ref[...], wout_ref[...])


def _sconv(x2, batch, seq, g, w_in, conv_w, w_out, tm=1024, tc=256):
    n = batch * seq
    nt = seq // tm
    return pl.pallas_call(
        functools.partial(_sconv_body, tc=tc),
        grid=(batch, nt),
        in_specs=[
            pl.BlockSpec((tm, D_MODEL), lambda b, t: (b * nt + t, 0)),
            _const_spec((1, D_MODEL)),
            _const_spec((D_MODEL, 3 * D_MODEL)),
            _const_spec((3, D_MODEL)),
            _const_spec((D_MODEL, D_MODEL)),
        ],
        out_specs=pl.BlockSpec((tm, D_MODEL), lambda b, t: (b * nt + t, 0)),
        out_shape=jax.ShapeDtypeStruct((n, D_MODEL), F32),
        scratch_shapes=[pltpu.VMEM((tm, D_MODEL), BF16), pltpu.VMEM((tm, D_MODEL), BF16),
                        pltpu.VMEM((8, D_MODEL), F32)],
        compiler_params=_params("arbitrary", "arbitrary"),
        name="sconv",
    )(x2, g, w_in, conv_w, w_out)


def _block_diag_ones(n):
    i = jnp.arange(n) // HEAD_DIM
    return (i[:, None] == i[None, :]).astype(BF16)


def _rwkv_moba_mixer(x2, batch, seq, rel_bias, mix_norm, w_in, shift_mu, w0, w_lora_up, a0, a_lora_up, g_lora_up,
                     k_k, k_a, r_k, lnx_w, lnx_b, q_norm, k_norm, w_out):
    row = lambda z: z.reshape(1, -1).astype(F32)
    w_in = w_in.astype(BF16)
    q0 = RW_COLS
    w_rw = w_in[:, :q0]
    w_qt = w_in[:, q0:q0 + ATT_DIM].T
    w_k = w_in[:, q0 + ATT_DIM:q0 + 2 * ATT_DIM]
    w_vt = w_in[:, q0 + 2 * ATT_DIM:].T
    p_rw, k_att, q_t, v_t = _inproj0(x2, row(mix_norm), w_rw, w_k, w_qt, w_vt)

    zeros = jnp.zeros_like(w_lora_up)
    w_wa = jnp.concatenate([jnp.concatenate([w_lora_up, zeros], axis=1),
                            jnp.concatenate([zeros, a_lora_up], axis=1)], axis=0).astype(BF16)
    y_rw = _rwkv(p_rw, batch, seq, row(shift_mu), row(w0), row(a0), w_wa, g_lora_up.astype(BF16),
                 row(k_k), row(k_a), row(r_k), row(lnx_w), row(lnx_b), _block_diag_ones(RW_DIM),
                 ns=4 if batch % 4 == 0 else 1)

    bias_t = _t5bias(rel_bias.T.astype(F32), seq // MOBA_BLOCK)
    qg = jnp.tile(q_norm.astype(F32), 2).reshape(LANES, 1)
    kg = jnp.tile(k_norm.astype(F32), 2).reshape(1, LANES)
    y_att = _moba(k_att, q_t, v_t, bias_t, qg, kg, _block_diag_ones(LANES), batch, seq)

    w_out = w_out.astype(BF16)
    return _outproj(x2, y_rw, y_att, w_out[:RW_DIM], w_out[RW_DIM:])


def kernel(x, rel_bias, l0_mix_norm, l0_w_in, l0_shift_mu, l0_w0, l0_w_lora_up, l0_a0, l0_a_lora_up, l0_g_lora_up, l0_k_k, l0_k_a, l0_r_k, l0_lnx_w, l0_lnx_b, l0_q_norm, l0_k_norm, l0_w_out, l0_ffn_norm, l0_ffn_up, l0_ffn_conv_w, l0_ffn_conv_b, l0_ffn_down, l1_mix_norm, l1_w_in, l1_conv_w, l1_w_out, l1_ffn_norm, l1_ffn_up, l1_ffn_conv_w, l1_ffn_conv_b, l1_ffn_down):
    batch, seq, d = x.shape
    row = lambda z: z.reshape(1, -1).astype(F32)
    x2 = x.reshape(batch * seq, d)
    x2 = _rwkv_moba_mixer(x2, batch, seq, rel_bias, l0_mix_norm, l0_w_in, l0_shift_mu, l0_w0, l0_w_lora_up, l0_a0,
                          l0_a_lora_up, l0_g_lora_up, l0_k_k, l0_k_a, l0_r_k, l0_lnx_w, l0_lnx_b, l0_q_norm,
                          l0_k_norm, l0_w_out)
    x2 = _ffn(x2, batch, seq, row(l0_ffn_norm), l0_ffn_up.astype(BF16), l0_ffn_conv_w.astype(F32),
              row(l0_ffn_conv_b), l0_ffn_down.astype(BF16))
    x2 = _sconv(x2, batch, seq, row(l1_mix_norm), l1_w_in.astype(BF16), l1_conv_w.astype(F32),
                l1_w_out.astype(BF16))
    x2 = _ffn(x2, batch, seq, row(l1_ffn_norm), l1_ffn_up.astype(BF16), l1_ffn_conv_w.astype(F32),
              row(l1_ffn_conv_b), l1_ffn_down.astype(BF16))
    return x2.reshape(batch, seq, d)
```

```python
import functools
import math

import jax
import jax.numpy as jnp
from jax import lax
from jax.experimental import pallas as pl
from jax.experimental.pallas import tpu as pltpu

F32 = jnp.float32
BF16 = jnp.bfloat16

D_MODEL = 1024
HEAD_DIM = 64
RW_DIM = 512
RW_COLS = 3 * RW_DIM + 64 + 64 + 128
ATT_DIM = 512
MOBA_BLOCK = 256
MOBA_TOPK = 3
NUM_BUCKETS = 32
REL_MAX_DIST = 1024
D_FF = 2816
NORM_EPS = 1e-6
LNX_EPS = 64e-5
NEG_INF = -1e30
LOG2E = math.log2(math.e)
MAX_BRACKET = 160.0
BOUND_SLACK = 1.02

CHUNK = 64
LANES = 128
VMEM_LIMIT = 52 * 1024 * 1024


def _dot(a, b):
    return jnp.dot(a, b, preferred_element_type=F32)


def _dot_nt(a, b):
    return lax.dot_general(a, b, (((1,), (1,)), ((), ())), preferred_element_type=F32)


def _dot_split(x, w):
    hi = x.astype(BF16)
    lo = (x - hi.astype(F32)).astype(BF16)
    return _dot(hi, w) + _dot(lo, w)


def _rms_norm_rows(x, g):
    return x * lax.rsqrt(jnp.mean(x * x, axis=-1, keepdims=True) + NORM_EPS) * g


def _const_spec(shape):
    return pl.BlockSpec(shape, lambda *_: (0,) * len(shape), pipeline_mode=pl.Buffered(1))


def _params(*sem):
    return pltpu.CompilerParams(dimension_semantics=sem, vmem_limit_bytes=VMEM_LIMIT)


def _inproj0_body(x_ref, g_ref, wrw_ref, wk_ref, wqt_ref, wvt_ref, prw_ref, k_ref, qt_ref, vt_ref):
    xn = _rms_norm_rows(x_ref[...], g_ref[...]).astype(BF16)
    prw_ref[...] = _dot(xn, wrw_ref[...])
    k_ref[...] = _dot(xn, wk_ref[...]).astype(BF16)
    qt_ref[...] = _dot_nt(wqt_ref[...], xn).astype(BF16)
    vt_ref[...] = _dot_nt(wvt_ref[...], xn).astype(BF16)


def _inproj0(x2, g, w_rw, w_k, w_qt, w_vt, tm=1024):
    n = x2.shape[0]
    return pl.pallas_call(
        _inproj0_body,
        grid=(n // tm,),
        in_specs=[
            pl.BlockSpec((tm, D_MODEL), lambda i: (i, 0)),
            _const_spec((1, D_MODEL)),
            _const_spec((D_MODEL, RW_COLS)),
            _const_spec((D_MODEL, ATT_DIM)),
            _const_spec((ATT_DIM, D_MODEL)),
            _const_spec((ATT_DIM, D_MODEL)),
        ],
        out_specs=[
            pl.BlockSpec((tm, RW_COLS), lambda i: (i, 0)),
            pl.BlockSpec((tm, ATT_DIM), lambda i: (i, 0)),
            pl.BlockSpec((ATT_DIM, tm), lambda i: (0, i)),
            pl.BlockSpec((ATT_DIM, tm), lambda i: (0, i)),
        ],
        out_shape=[
            jax.ShapeDtypeStruct((n, RW_COLS), F32),
            jax.ShapeDtypeStruct((n, ATT_DIM), BF16),
            jax.ShapeDtypeStruct((ATT_DIM, n), BF16),
            jax.ShapeDtypeStruct((ATT_DIM, n), BF16),
        ],
        compiler_params=_params("arbitrary"),
        name="inproj0",
    )(x2, g, w_rw, w_k, w_qt, w_vt)


def _rwkv_body(p_ref, mu_ref, w0_ref, a0_ref, wwa_ref, gup_ref, kk_ref, ka_ref, rk_ref, lnw_ref, lnb_ref,
               tri_ref, ones_ref, y_ref, carry_ref, state_ref):
    L = CHUNK
    ns = p_ref.shape[0]
    rows = [slice(s * L, (s + 1) * L) for s in range(ns)]

    @pl.when(pl.program_id(1) == 0)
    def _():
        carry_ref[...] = jnp.zeros_like(carry_ref)
        state_ref[...] = jnp.zeros_like(state_ref)

    p = p_ref[...].reshape(ns * L, RW_COLS)
    row = lax.broadcasted_iota(jnp.int32, p.shape, 0)
    prev = pltpu.roll(p, 1, axis=0)
    for s in range(ns):
        prev = jnp.where(row == s * L, carry_ref[s:s + 1, :], prev)
    for s in range(ns):
        carry_ref[s:s + 1, :] = p[(s + 1) * L - 1:(s + 1) * L, :]
    ps = p + (prev - p) * mu_ref[...]
    r = ps[:, 0:RW_DIM]
    k = ps[:, RW_DIM:2 * RW_DIM]
    v = ps[:, 2 * RW_DIM:3 * RW_DIM]
    slab = ps[:, 3 * RW_DIM:3 * RW_DIM + LANES]
    dg = ps[:, 3 * RW_DIM + LANES:RW_COLS]
    lo = lax.broadcasted_iota(jnp.int32, (L, LANES), 1) < HEAD_DIM
    lo_all = lax.broadcasted_iota(jnp.int32, slab.shape, 1) < HEAD_DIM

    wa = _dot(jnp.where(lo_all, jnp.tanh(slab), slab).astype(BF16), wwa_ref[...])
    ld = -math.exp(-0.5) * jax.nn.sigmoid(w0_ref[...] + wa[:, :RW_DIM])
    a = jax.nn.sigmoid(a0_ref[...] + wa[:, RW_DIM:])
    g = _dot(jax.nn.sigmoid(dg).astype(BF16), gup_ref[...])
    ones_bd = ones_ref[...]
    kk = k * kk_ref[...]
    kk = kk * lax.rsqrt(jnp.maximum(_dot((kk * kk).astype(BF16), ones_bd), 1e-24))
    kmod = k * (1.0 + (a - 1.0) * ka_ref[...])

    tri = tri_ref[...]
    ld_hi = ld.astype(BF16)
    rem = ld - ld_hi.astype(F32)
    ld_mid = rem.astype(BF16)
    ld_lo = (rem - ld_mid.astype(F32)).astype(BF16)
    cs = _dot(tri, ld_hi) + _dot(tri, ld_mid) + _dot(tri, ld_lo)
    e_c = jnp.exp(cs)
    e_nc = jnp.exp(-cs)
    abar = -kk * jnp.exp(cs - ld)
    rbar = r * e_c
    bbar = kk * a * e_nc
    kbar = kmod * e_nc
    wl = [e_c[(s + 1) * L - 1:(s + 1) * L, :] for s in range(ns)]

    ri = lax.broadcasted_iota(jnp.int32, (LANES, LANES), 0)
    ci = lax.broadcasted_iota(jnp.int32, (LANES, LANES), 1)
    same = (ri >= HEAD_DIM) == (ci >= HEAD_DIM)
    t_i = lax.broadcasted_iota(jnp.int32, (L, LANES), 0)
    s_i = lax.broadcasted_iota(jnp.int32, (L, LANES), 1) & (L - 1)
    m_strict = s_i < t_i
    m_incl = s_i <= t_i
    m_incl2 = jnp.concatenate([m_incl, m_incl], axis=1)
    eye = s_i == t_i

    def stack(z):
        return jnp.concatenate([jnp.where(lo, z, 0.0), jnp.where(lo, 0.0, z)], axis=0).astype(BF16)

    npair = RW_DIM // LANES
    pairs = range(ns * npair)
    sls = [(rows[n // npair], slice(LANES * (n % npair), LANES * (n % npair + 1))) for n in pairs]
    ab = [abar[sl] for sl in sls]
    rb = [rbar[sl] for sl in sls]
    vv = [v[sl] for sl in sls]
    s0 = [state_ref[gp] for gp in pairs]
    abrb = [jnp.concatenate([ab[gp], rb[gp]], axis=0).astype(BF16) for gp in pairs]
    q4 = [_dot_nt(abrb[gp], jnp.concatenate([stack(bbar[sls[gp]]), stack(kbar[sls[gp]])], axis=0))
          for gp in pairs]
    ar = [_dot_nt(abrb[gp], s0[gp].astype(BF16)) for gp in pairs]
    a_ab = [jnp.where(m_strict, q[0:L, 0:LANES], 0.0) for q in q4]
    tinv = [jnp.where(eye, 1.0, a) for a in a_ab]
    pw = [_dot(a.astype(BF16), stack(a)) for a in a_ab]
    for _ in range(4):
        both = [_dot(jnp.concatenate([tinv[gp], pw[gp]], axis=0).astype(BF16), stack(pw[gp])) for gp in pairs]
        tinv = [tinv[gp] + both[gp][0:L] for gp in pairs]
        pw = [both[gp][L:] for gp in pairs]
    tinv = [tinv[gp] + _dot(tinv[gp].astype(BF16), stack(pw[gp])) for gp in pairs]
    v_st = [stack(z) for z in vv]
    rhs = [ar[gp][0:L] + _dot(jnp.where(m_strict, q4[gp][0:L, LANES:], 0.0).astype(BF16), v_st[gp])
           for gp in pairs]
    u = [_dot(tinv[gp].astype(BF16), stack(rhs[gp])) for gp in pairs]
    ys = [ar[gp][L:] + _dot(jnp.where(m_incl2, q4[gp][L:], 0.0).astype(BF16),
                            jnp.concatenate([stack(u[gp]), v_st[gp]], axis=0)) for gp in pairs]
    for gp in pairs:
        wl_gp = wl[gp // npair][:, sls[gp][1]]
        uv = jnp.concatenate([u[gp], vv[gp]], axis=0)
        bk = jnp.concatenate([bbar[sls[gp]], kbar[sls[gp]]], axis=0) * wl_gp
        s_add = _dot(uv.T.astype(BF16), bk.astype(BF16))
        state_ref[gp] = s0[gp] * wl_gp + jnp.where(same, s_add, 0.0)
    y = jnp.concatenate([jnp.concatenate(ys[s * npair:(s + 1) * npair], axis=1) for s in range(ns)], axis=0)

    inv_n = 1.0 / HEAD_DIM
    d = y - _dot_split(y, ones_bd) * inv_n
    var = _dot((d * d).astype(BF16), ones_bd) * inv_n
    yn = d * lax.rsqrt(var + LNX_EPS) * lnw_ref[...] + lnb_ref[...]
    bonus = _dot((r * kmod * rk_ref[...]).astype(BF16), ones_bd) * v
    y_ref[...] = ((yn + bonus) * g).astype(BF16).reshape(ns, L, RW_DIM)


def _rwkv(p_rw, batch, seq, mu, w0, a0, w_wa, g_up, k_k, k_a, r_k, lnx_w, lnx_b, ones_bd, ns=2):
    nc = seq // CHUNK
    vec = lambda: _const_spec((1, RW_DIM))
    idx = jnp.arange(ns * CHUNK)
    tri = ((idx[:, None] >= idx[None, :]) & (idx[:, None] // CHUNK == idx[None, :] // CHUNK)).astype(BF16)
    y = pl.pallas_call(
        _rwkv_body,
        grid=(batch // ns, nc),
        in_specs=[
            pl.BlockSpec((ns, CHUNK, RW_COLS), lambda b, c: (b, c, 0)),
            _const_spec((1, RW_COLS)), vec(), vec(),
            _const_spec((LANES, 2 * RW_DIM)), _const_spec((LANES, RW_DIM)),
            vec(), vec(), vec(), vec(), vec(),
            _const_spec((ns * CHUNK, ns * CHUNK)), _const_spec((RW_DIM, RW_DIM)),
        ],
        out_specs=pl.BlockSpec((ns, CHUNK, RW_DIM), lambda b, c: (b, c, 0)),
        out_shape=jax.ShapeDtypeStruct((batch, seq, RW_DIM), BF16),
        scratch_shapes=[
            pltpu.VMEM((8, RW_COLS), F32),
            pltpu.VMEM((ns * RW_DIM // LANES, LANES, LANES), F32),
        ],
        compiler_params=_params("arbitrary", "arbitrary"),
        name="rwkv",
    )(p_rw.reshape(batch, seq, RW_COLS), mu, w0, a0, w_wa, g_up, k_k, k_a, r_k, lnx_w, lnx_b, tri, ones_bd)
    return y.reshape(batch * seq, RW_DIM)


def _bucket_ranges():
    max_exact = NUM_BUCKETS // 2
    ratio_num, ratio_den = REL_MAX_DIST, max_exact
    bounds = []
    for b in range(NUM_BUCKETS):
        if b < max_exact:
            bounds.append((b, b + 1))
            continue
        def first(j):
            n = max_exact
            while (n ** (NUM_BUCKETS - max_exact)) * (ratio_den ** j) < (ratio_num ** j) * (max_exact ** (NUM_BUCKETS - max_exact)):
                n += 1
            return n
        lo = first(b - max_exact)
        hi = first(b + 1 - max_exact) if b < NUM_BUCKETS - 1 else 1 << 30
        bounds.append((lo, hi))
    return bounds


def _t5bias_body(tab_ref, o_ref, bmax_ref, *, ranges):
    h = pl.program_id(0)
    blk = MOBA_BLOCK
    bmax = tab_ref[h, 0]
    for b in range(1, NUM_BUCKETS):
        bmax = jnp.maximum(bmax, tab_ref[h, b])
    bmax_ref[...] = jnp.full(bmax_ref.shape, bmax * LOG2E, F32)
    rel = lax.broadcasted_iota(jnp.int32, (blk, blk), 1) - lax.broadcasted_iota(jnp.int32, (blk, blk), 0)
    nblk = o_ref.shape[1] // blk
    for di in range(nblk):
        dist = rel + di * blk
        n = jnp.maximum(dist, 0)
        acc = jnp.zeros((blk, blk), F32)
        for b, (lo, hi) in enumerate(ranges):
            if hi <= di * blk - (blk - 1) or lo > di * blk + (blk - 1):
                continue
            acc = jnp.where((n >= lo) & (n < hi), tab_ref[h, b] * LOG2E, acc)
        o_ref[0, (nblk - 1 - di) * blk:(nblk - di) * blk, :] = jnp.where(dist >= 0, acc, NEG_INF)


def _t5bias(table, nblk):
    heads = table.shape[0]
    return pl.pallas_call(
        functools.partial(_t5bias_body, ranges=_bucket_ranges()),
        grid=(heads,),
        in_specs=[pl.BlockSpec(memory_space=pltpu.SMEM)],
        out_specs=[pl.BlockSpec((1, nblk * MOBA_BLOCK, MOBA_BLOCK), lambda h: (h, 0, 0)),
                   pl.BlockSpec((1, 8, LANES), lambda h: (h, 0, 0))],
        out_shape=[jax.ShapeDtypeStruct((heads, nblk * MOBA_BLOCK, MOBA_BLOCK), F32),
                   jax.ShapeDtypeStruct((heads, 8, LANES), F32)],
        compiler_params=_params("arbitrary"),
        name="t5bias",
    )(table)


def _moba_body(k_ref, qt_ref, vt_ref, bias_ref, bmax_ref, qg_ref, kg_ref, ones_ref, o_ref, kaug_ref, qaug_ref, vaug_ref,
               ot_ref):
    seq = k_ref.shape[0]
    blk = MOBA_BLOCK
    nb = seq // blk

    k2 = k_ref[...].astype(F32)
    ms = _dot((k2 * k2).astype(BF16), ones_ref[...]) * (1.0 / HEAD_DIM)
    for hh in range(2):
        vaug_ref[hh, 0:HEAD_DIM, :] = vt_ref[hh * HEAD_DIM:(hh + 1) * HEAD_DIM, :]
        vaug_ref[hh, HEAD_DIM:, :] = jnp.ones((vaug_ref.shape[1] - HEAD_DIM, seq), BF16)
    knf = k2 * lax.rsqrt(ms + NORM_EPS) * kg_ref[...]
    kmean = jnp.concatenate(
        [jnp.mean(knf[j * blk:(j + 1) * blk], axis=0, keepdims=True) for j in range(nb)], axis=0)
    klane = lax.broadcasted_iota(jnp.int32, (seq, LANES), 1)
    kblk = lax.broadcasted_iota(jnp.int32, (seq, LANES), 0) // blk
    kaug_ref[0] = jnp.where(klane < HEAD_DIM, knf, jnp.where(klane - HEAD_DIM == kblk, 1.0, 0.0)).astype(BF16)
    kaug_ref[1] = jnp.where(klane >= HEAD_DIM, knf, jnp.where(klane == kblk, 1.0, 0.0)).astype(BF16)

    q2 = qt_ref[...].astype(F32)

    def nrm(qh):
        return qh * lax.rsqrt(jnp.mean(qh * qh, axis=0, keepdims=True) + NORM_EPS)

    qn = jnp.concatenate([nrm(q2[0:HEAD_DIM]), nrm(q2[HEAD_DIM:])], axis=0) * qg_ref[...] \
        * (HEAD_DIM ** -0.5 * LOG2E)
    qnb = qn.astype(BF16)

    km_hi = kmean.astype(BF16)
    km_lo = (kmean - km_hi.astype(F32)).astype(BF16)
    lane = lax.broadcasted_iota(jnp.int32, (nb, LANES), 1)
    rowi = lax.broadcasted_iota(jnp.int32, (nb, seq), 0)
    qblk = lax.broadcasted_iota(jnp.int32, (nb, seq), 1) // blk
    past = rowi < qblk
    qblk1 = qblk[0:1]
    zero_b = jnp.zeros_like(km_hi)
    pad = jnp.zeros((HEAD_DIM - nb, seq), F32)

    qk_max = jnp.max(jnp.abs(qg_ref[...]), axis=0, keepdims=True) \
        * jnp.max(jnp.abs(kg_ref[...]), axis=1, keepdims=True) * (HEAD_DIM ** 0.5 * LOG2E * BOUND_SLACK)
    qk_own = qn * knf.T
    gaps = []

    for hh in range(2):
        hs = slice(hh * HEAD_DIM, (hh + 1) * HEAD_DIM)
        own_bias = bias_ref[hh, (nb - 1) * blk:(nb - 1) * blk + 1, 0:1]
        ub = qk_max + bmax_ref[hh, 0:1, 0:1]
        lb = jnp.sum(qk_own[hs], axis=0, keepdims=True) + own_bias
        gaps.append(ub - lb)
        shift = 0.5 * (ub + lb)

        hm = (lane < HEAD_DIM) if hh == 0 else (lane >= HEAD_DIM)
        gate = _dot(jnp.where(hm, km_hi, zero_b), qnb) + _dot(jnp.where(hm, km_lo, zero_b), qnb)
        rows = []
        for j in range(nb):
            gj = gate[j:j + 1, :]
            beats = (gate > gj) | ((gate == gj) & (rowi < j))
            cnt = jnp.sum(jnp.where(past & beats, 1.0, 0.0), axis=0, keepdims=True)
            sel = ((qblk1 > j) & (cnt < MOBA_TOPK)) | (qblk1 == j)
            rows.append(jnp.where(sel, 0.0, NEG_INF))
        selb = jnp.concatenate(rows, axis=0)
        for variant, sb in ((0, selb), (2, selb - shift)):
            if hh == 0:
                qaug_ref[variant + hh] = jnp.concatenate([qn[0:HEAD_DIM], sb, pad], axis=0).astype(BF16)
            else:
                qaug_ref[variant + hh] = jnp.concatenate([sb, pad, qn[HEAD_DIM:]], axis=0).astype(BF16)

    tiles = [(hh, i) for i in range(nb) for hh in range(2)]

    def scores(variant, hh, i):
        nk = (i + 1) * blk
        return _dot(kaug_ref[hh, 0:nk, :], qaug_ref[variant + hh, :, i * blk:(i + 1) * blk]) \
            + bias_ref[hh, (nb - 1 - i) * blk:, :]

    def attend(variant, probs):
        s_next = scores(variant, *tiles[0])
        for n, (hh, i) in enumerate(tiles):
            s = s_next
            if n + 1 < len(tiles):
                s_next = scores(variant, *tiles[n + 1])
            acc = _dot(vaug_ref[hh, :, 0:(i + 1) * blk], probs(s))
            ot_ref[hh * HEAD_DIM:(hh + 1) * HEAD_DIM, i * blk:(i + 1) * blk] = \
                acc[0:HEAD_DIM] / acc[HEAD_DIM:HEAD_DIM + 1]

    narrow = jnp.max(jnp.maximum(gaps[0], gaps[1])) <= MAX_BRACKET

    @pl.when(narrow)
    def _():
        attend(2, lambda s: jnp.exp2(s).astype(BF16))

    @pl.when(jnp.logical_not(narrow))
    def _():
        attend(0, lambda s: jnp.exp2(s - jnp.max(s, axis=0, keepdims=True)).astype(BF16))

    for i in range(nb):
        o_ref[i * blk:(i + 1) * blk, :] = ot_ref[:, i * blk:(i + 1) * blk].T.astype(BF16)


def _moba(k_att, q_t, v_t, bias_t, bias_max, qg, kg, ones_bd, batch, seq):
    n = batch * seq
    npair = ATT_DIM // LANES
    nb = seq // MOBA_BLOCK
    return pl.pallas_call(
        _moba_body,
        grid=(npair, batch),
        in_specs=[
            pl.BlockSpec((seq, LANES), lambda g, b: (b, g)),
            pl.BlockSpec((LANES, seq), lambda g, b: (g, b)),
            pl.BlockSpec((LANES, seq), lambda g, b: (g, b)),
            pl.BlockSpec((2, seq, MOBA_BLOCK), lambda g, b: (g, 0, 0)),
            pl.BlockSpec((2, 8, LANES), lambda g, b: (g, 0, 0)),
            _const_spec((LANES, 1)), _const_spec((1, LANES)), _const_spec((LANES, LANES)),
        ],
        out_specs=pl.BlockSpec((seq, LANES), lambda g, b: (b, g)),
        out_shape=jax.ShapeDtypeStruct((n, ATT_DIM), BF16),
        scratch_shapes=[
            pltpu.VMEM((2, seq, LANES), BF16),
            pltpu.VMEM((4, LANES, seq), BF16),
            pltpu.VMEM((2, HEAD_DIM + 16, seq), BF16),
            pltpu.VMEM((LANES, seq), F32),
        ],
        compiler_params=_params("arbitrary", "arbitrary"),
        name="moba",
    )(k_att, q_t, v_t, bias_t, bias_max, qg, kg, ones_bd)


def _outproj_body(x_ref, yr_ref, ya_ref, wr_ref, wa_ref, o_ref):
    o_ref[...] = x_ref[...] + _dot(yr_ref[...], wr_ref[...]) + _dot(ya_ref[...], wa_ref[...])


def _outproj(x2, y_rw, y_att, w_r, w_a, tm=2048):
    n = x2.shape[0]
    return pl.pallas_call(
        _outproj_body,
        grid=(n // tm,),
        in_specs=[
            pl.BlockSpec((tm, D_MODEL), lambda i: (i, 0)),
            pl.BlockSpec((tm, RW_DIM), lambda i: (i, 0)),
            pl.BlockSpec((tm, ATT_DIM), lambda i: (i, 0)),
            _const_spec((RW_DIM, D_MODEL)), _const_spec((ATT_DIM, D_MODEL)),
        ],
        out_specs=pl.BlockSpec((tm, D_MODEL), lambda i: (i, 0)),
        out_shape=jax.ShapeDtypeStruct((n, D_MODEL), F32),
        compiler_params=_params("arbitrary"),
        name="outproj",
    )(x2, y_rw, y_att, w_r, w_a)


def _causal_conv3(h, carry_ref, cols, w):
    tm = h.shape[0]
    c = carry_ref[:, cols]
    row = lax.broadcasted_iota(jnp.int32, c.shape, 0)
    r1 = pltpu.roll(h, 1, axis=0)
    r2 = pltpu.roll(h, 2, axis=0)
    h1 = jnp.concatenate([jnp.where(row == 0, c[7:8], r1[0:8]), r1[8:]], axis=0)
    h2 = jnp.concatenate([jnp.where(row == 0, c[6:7], jnp.where(row == 1, c[7:8], r2[0:8])), r2[8:]], axis=0)
    carry_ref[:, cols] = h[tm - 8:tm]
    return w[2:3] * h + w[1:2] * h1 + w[0:1] * h2


def _ffn_body(x_ref, g_ref, up_ref, cw_ref, cb_ref, down_ref, o_ref, xn_ref, z_ref, carry_ref, *, tc):
    @pl.when(pl.program_id(1) == 0)
    def _():
        carry_ref[...] = jnp.zeros_like(carry_ref)

    xn_ref[...] = _rms_norm_rows(x_ref[...], g_ref[...]).astype(BF16)

    def branch(c0):
        cols = slice(c0, c0 + tc)
        h = _dot(xn_ref[...], up_ref[:, cols])
        return _causal_conv3(h, carry_ref, cols, cw_ref[:, cols]) + cb_ref[:, cols]

    for c0 in range(0, D_FF, tc):
        za = branch(c0)
        zu = branch(D_FF + c0)
        z_ref[:, c0:c0 + tc] = (za * jax.nn.sigmoid(za) * zu).astype(BF16)
    o_ref[...] = x_ref[...] + _dot(z_ref[...], down_ref[...])


def _ffn(x2, batch, seq, g, up, conv_w, conv_b, down, tm=1024, tc=256):
    n = batch * seq
    nt = seq // tm
    return pl.pallas_call(
        functools.partial(_ffn_body, tc=tc),
        grid=(batch, nt),
        in_specs=[
            pl.BlockSpec((tm, D_MODEL), lambda b, t: (b * nt + t, 0)),
            _const_spec((1, D_MODEL)),
            _const_spec((D_MODEL, 2 * D_FF)),
            _const_spec((3, 2 * D_FF)),
            _const_spec((1, 2 * D_FF)),
            _const_spec((D_FF, D_MODEL)),
        ],
        out_specs=pl.BlockSpec((tm, D_MODEL), lambda b, t: (b * nt + t, 0)),
        out_shape=jax.ShapeDtypeStruct((n, D_MODEL), F32),
        scratch_shapes=[pltpu.VMEM((tm, D_MODEL), BF16), pltpu.VMEM((tm, D_FF), BF16),
                        pltpu.VMEM((8, 2 * D_FF), F32)],
        compiler_params=_params("arbitrary", "arbitrary"),
        name="ffn",
    )(x2, g, up, conv_w, conv_b, down)


def _sconv_body(x_ref, g_ref, win_ref, cw_ref, wout_ref, o_ref, xn_ref, z_ref, carry_ref, *, tc):
    @pl.when(pl.program_id(1) == 0)
    def _():
        carry_ref[...] = jnp.zeros_like(carry_ref)

    xn_ref[...] = _rms_norm_rows(x_ref[...], g_ref[...]).astype(BF16)

    for c0 in range(0, D_MODEL, tc):
        cols = slice(c0, c0 + tc)
        xn = xn_ref[...]
        b_gate = _dot(xn, win_ref[:, cols])
        c_gate = _dot(xn, win_ref[:, D_MODEL + c0:D_MODEL + c0 + tc])
        hid = _dot(xn, win_ref[:, 2 * D_MODEL + c0:2 * D_MODEL + c0 + tc])
        conv = _causal_conv3(c_gate * hid, carry_ref, cols, cw_ref[:, cols])
        z_ref[:, cols] = (b_gate * conv).astype(BF16)
    o_ref[...] = x_ref[...] + _dot(z_ref[...], wout_ref[...])


def _sconv(x2, batch, seq, g, w_in, conv_w, w_out, tm=1024, tc=256):
    n = batch * seq
    nt = seq // tm
    return pl.pallas_call(
        functools.partial(_sconv_body, tc=tc),
        grid=(batch, nt),
        in_specs=[
            pl.BlockSpec((tm, D_MODEL), lambda b, t: (b * nt + t, 0)),
            _const_spec((1, D_MODEL)),
            _const_spec((D_MODEL, 3 * D_MODEL)),
            _const_spec((3, D_MODEL)),
            _const_spec((D_MODEL, D_MODEL)),
        ],
        out_specs=pl.BlockSpec((tm, D_MODEL), lambda b, t: (b * nt + t, 0)),
        out_shape=jax.ShapeDtypeStruct((n, D_MODEL), F32),
        scratch_shapes=[pltpu.VMEM((tm, D_MODEL), BF16), pltpu.VMEM((tm, D_MODEL), BF16),
                        pltpu.VMEM((8, D_MODEL), F32)],
        compiler_params=_params("arbitrary", "arbitrary"),
        name="sconv",
    )(x2, g, w_in, conv_w, w_out)


def _block_diag_ones(n):
    i = jnp.arange(n) // HEAD_DIM
    return (i[:, None] == i[None, :]).astype(BF16)


def _rwkv_moba_mixer(x2, batch, seq, rel_bias, mix_norm, w_in, shift_mu, w0, w_lora_up, a0, a_lora_up, g_lora_up,
                     k_k, k_a, r_k, lnx_w, lnx_b, q_norm, k_norm, w_out):
    row = lambda z: z.reshape(1, -1).astype(F32)
    w_in = w_in.astype(BF16)
    q0 = RW_COLS
    w_rw = w_in[:, :q0]
    w_qt = w_in[:, q0:q0 + ATT_DIM].T
    w_k = w_in[:, q0 + ATT_DIM:q0 + 2 * ATT_DIM]
    w_vt = w_in[:, q0 + 2 * ATT_DIM:].T
    p_rw, k_att, q_t, v_t = _inproj0(x2, row(mix_norm), w_rw, w_k, w_qt, w_vt)

    zeros = jnp.zeros_like(w_lora_up)
    w_wa = jnp.concatenate([jnp.concatenate([w_lora_up, zeros], axis=1),
                            jnp.concatenate([zeros, a_lora_up], axis=1)], axis=0).astype(BF16)
    y_rw = _rwkv(p_rw, batch, seq, row(shift_mu), row(w0), row(a0), w_wa, g_lora_up.astype(BF16),
                 row(k_k), row(k_a), row(r_k), row(lnx_w), row(lnx_b), _block_diag_ones(RW_DIM),
                 ns=8 if batch % 8 == 0 else 1)

    bias_t, bias_max = _t5bias(rel_bias.T.astype(F32), seq // MOBA_BLOCK)
    qg = jnp.tile(q_norm.astype(F32), 2).reshape(LANES, 1)
    kg = jnp.tile(k_norm.astype(F32), 2).reshape(1, LANES)
    y_att = _moba(k_att, q_t, v_t, bias_t, bias_max, qg, kg, _block_diag_ones(LANES), batch, seq)

    w_out = w_out.astype(BF16)
    return _outproj(x2, y_rw, y_att, w_out[:RW_DIM], w_out[RW_DIM:])


def kernel(x, rel_bias, l0_mix_norm, l0_w_in, l0_shift_mu, l0_w0, l0_w_lora_up, l0_a0, l0_a_lora_up, l0_g_lora_up, l0_k_k, l0_k_a, l0_r_k, l0_lnx_w, l0_lnx_b, l0_q_norm, l0_k_norm, l0_w_out, l0_ffn_norm, l0_ffn_up, l0_ffn_conv_w, l0_ffn_conv_b, l0_ffn_down, l1_mix_norm, l1_w_in, l1_conv_w, l1_w_out, l1_ffn_norm, l1_ffn_up, l1_ffn_conv_w, l1_ffn_conv_b, l1_ffn_down):
    batch, seq, d = x.shape
    row = lambda z: z.reshape(1, -1).astype(F32)
    x2 = x.reshape(batch * seq, d)
    x2 = _rwkv_moba_mixer(x2, batch, seq, rel_bias, l0_mix_norm, l0_w_in, l0_shift_mu, l0_w0, l0_w_lora_up, l0_a0,
                          l0_a_lora_up, l0_g_lora_up, l0_k_k, l0_k_a, l0_r_k, l0_lnx_w, l0_lnx_b, l0_q_norm,
                          l0_k_norm, l0_w_out)
    x2 = _ffn(x2, batch, seq, row(l0_ffn_norm), l0_ffn_up.astype(BF16), l0_ffn_conv_w.astype(F32),
              row(l0_ffn_conv_b), l0_ffn_down.astype(BF16))
    x2 = _sconv(x2, batch, seq, row(l1_mix_norm), l1_w_in.astype(BF16), l1_conv_w.astype(F32),
                l1_w_out.astype(BF16))
    x2 = _ffn(x2, batch, seq, row(l1_ffn_norm), l1_ffn_up.astype(BF16), l1_ffn_conv_w.astype(F32),
              row(l1_ffn_conv_b), l1_ffn_down.astype(BF16))
    return x2.reshape(batch, seq, d)
```

```python
import functools
import math

import jax
import jax.numpy as jnp
from jax import lax
from jax.experimental import pallas as pl
from jax.experimental.pallas import tpu as pltpu

F32 = jnp.float32
BF16 = jnp.bfloat16

D_MODEL = 1024
HEAD_DIM = 64
RW_DIM = 512
RW_COLS = 3 * RW_DIM + 64 + 64 + 128
ATT_DIM = 512
MOBA_BLOCK = 256
MOBA_TOPK = 3
NUM_BUCKETS = 32
REL_MAX_DIST = 1024
D_FF = 2816
NORM_EPS = 1e-6
LNX_EPS = 64e-5
NEG_INF = -1e30
LOG2E = math.log2(math.e)
MAX_BRACKET = 160.0
BOUND_SLACK = 1.02

CHUNK = 64
LANES = 128
VMEM_LIMIT = 52 * 1024 * 1024


def _dot(a, b):
    return jnp.dot(a, b, preferred_element_type=F32)


def _dot_nt(a, b):
    return lax.dot_general(a, b, (((1,), (1,)), ((), ())), preferred_element_type=F32)


def _rms_norm_rows(x, g):
    return x * lax.rsqrt(jnp.mean(x * x, axis=-1, keepdims=True) + NORM_EPS) * g


def _const_spec(shape):
    return pl.BlockSpec(shape, lambda *_: (0,) * len(shape), pipeline_mode=pl.Buffered(1))


def _params(*sem):
    return pltpu.CompilerParams(dimension_semantics=sem, vmem_limit_bytes=VMEM_LIMIT)


def _inproj0_body(x_ref, g_ref, wrw_ref, wk_ref, wqt_ref, wvt_ref, prw_ref, k_ref, qt_ref, vt_ref):
    xn = _rms_norm_rows(x_ref[...], g_ref[...]).astype(BF16)
    prw_ref[...] = _dot(xn, wrw_ref[...])
    k_ref[...] = _dot(xn, wk_ref[...]).astype(BF16)
    qt_ref[...] = _dot_nt(wqt_ref[...], xn).astype(BF16)
    vt_ref[...] = _dot_nt(wvt_ref[...], xn).astype(BF16)


def _inproj0(x2, g, w_rw, w_k, w_qt, w_vt, tm=1024):
    n = x2.shape[0]
    return pl.pallas_call(
        _inproj0_body,
        grid=(n // tm,),
        in_specs=[
            pl.BlockSpec((tm, D_MODEL), lambda i: (i, 0)),
            _const_spec((1, D_MODEL)),
            _const_spec((D_MODEL, RW_COLS)),
            _const_spec((D_MODEL, ATT_DIM)),
            _const_spec((ATT_DIM, D_MODEL)),
            _const_spec((ATT_DIM, D_MODEL)),
        ],
        out_specs=[
            pl.BlockSpec((tm, RW_COLS), lambda i: (i, 0)),
            pl.BlockSpec((tm, ATT_DIM), lambda i: (i, 0)),
            pl.BlockSpec((ATT_DIM, tm), lambda i: (0, i)),
            pl.BlockSpec((ATT_DIM, tm), lambda i: (0, i)),
        ],
        out_shape=[
            jax.ShapeDtypeStruct((n, RW_COLS), F32),
            jax.ShapeDtypeStruct((n, ATT_DIM), BF16),
            jax.ShapeDtypeStruct((ATT_DIM, n), BF16),
            jax.ShapeDtypeStruct((ATT_DIM, n), BF16),
        ],
        compiler_params=_params("arbitrary"),
        name="inproj0",
    )(x2, g, w_rw, w_k, w_qt, w_vt)


def _rwkv_body(p_ref, mu_ref, w0_ref, a0_ref, wwa_ref, gup_ref, kk_ref, ka_ref, rk_ref, lnw_ref, lnb_ref,
               tri_ref, ones_ref, y_ref, carry_ref, state_ref):
    L = CHUNK
    ns = p_ref.shape[0]
    rows = [slice(s * L, (s + 1) * L) for s in range(ns)]

    @pl.when(pl.program_id(1) == 0)
    def _():
        carry_ref[...] = jnp.zeros_like(carry_ref)
        state_ref[...] = jnp.zeros_like(state_ref)

    p = p_ref[...].reshape(ns * L, RW_COLS)
    row = lax.broadcasted_iota(jnp.int32, p.shape, 0)
    prev = pltpu.roll(p, 1, axis=0)
    for s in range(ns):
        prev = jnp.where(row == s * L, carry_ref[s:s + 1, :], prev)
    for s in range(ns):
        carry_ref[s:s + 1, :] = p[(s + 1) * L - 1:(s + 1) * L, :]
    ps = p + (prev - p) * mu_ref[...]
    r = ps[:, 0:RW_DIM]
    k = ps[:, RW_DIM:2 * RW_DIM]
    v = ps[:, 2 * RW_DIM:3 * RW_DIM]
    slab = ps[:, 3 * RW_DIM:3 * RW_DIM + LANES]
    dg = ps[:, 3 * RW_DIM + LANES:RW_COLS]
    lo = lax.broadcasted_iota(jnp.int32, (L, LANES), 1) < HEAD_DIM
    lo_all = lax.broadcasted_iota(jnp.int32, slab.shape, 1) < HEAD_DIM

    wa = _dot(jnp.where(lo_all, jnp.tanh(slab), slab).astype(BF16), wwa_ref[...])
    ld = -math.exp(-0.5) * jax.nn.sigmoid(w0_ref[...] + wa[:, :RW_DIM])
    a = jax.nn.sigmoid(a0_ref[...] + wa[:, RW_DIM:])
    g = _dot(jax.nn.sigmoid(dg).astype(BF16), gup_ref[...])
    ones_bd = ones_ref[...]
    kk = k * kk_ref[...]
    kk = kk * lax.rsqrt(jnp.maximum(_dot((kk * kk).astype(BF16), ones_bd), 1e-24))
    kmod = k * (1.0 + (a - 1.0) * ka_ref[...])

    tri = tri_ref[...]
    ld_hi = ld.astype(BF16)
    ld_lo = (ld - ld_hi.astype(F32)).astype(BF16)
    cs = jnp.concatenate([_dot(tri, ld_hi[rs]) + _dot(tri, ld_lo[rs]) for rs in rows], axis=0)
    e_c = jnp.exp(cs)
    e_nc = jnp.exp(-cs)
    abar = -kk * jnp.exp(cs - ld)
    rbar = r * e_c
    bbar = kk * a * e_nc
    kbar = kmod * e_nc
    wl = [e_c[(s + 1) * L - 1:(s + 1) * L, :] for s in range(ns)]

    ri = lax.broadcasted_iota(jnp.int32, (LANES, LANES), 0)
    ci = lax.broadcasted_iota(jnp.int32, (LANES, LANES), 1)
    same = (ri >= HEAD_DIM) == (ci >= HEAD_DIM)
    t_i = lax.broadcasted_iota(jnp.int32, (L, LANES), 0)
    s_i = lax.broadcasted_iota(jnp.int32, (L, LANES), 1) & (L - 1)
    m_strict = s_i < t_i
    m_incl = s_i <= t_i
    m_incl2 = jnp.concatenate([m_incl, m_incl], axis=1)
    eye = s_i == t_i

    def stack(z):
        return jnp.concatenate([jnp.where(lo, z, 0.0), jnp.where(lo, 0.0, z)], axis=0).astype(BF16)

    npair = RW_DIM // LANES
    pairs = range(ns * npair)
    sls = [(rows[n // npair], slice(LANES * (n % npair), LANES * (n % npair + 1))) for n in pairs]
    ab = [abar[sl] for sl in sls]
    rb = [rbar[sl] for sl in sls]
    vv = [v[sl] for sl in sls]
    s0 = [state_ref[gp] for gp in pairs]
    abrb = [jnp.concatenate([ab[gp], rb[gp]], axis=0).astype(BF16) for gp in pairs]
    q4 = [_dot_nt(abrb[gp], jnp.concatenate([stack(bbar[sls[gp]]), stack(kbar[sls[gp]])], axis=0))
          for gp in pairs]
    ar = [_dot_nt(abrb[gp], s0[gp].astype(BF16)) for gp in pairs]
    a_ab = [jnp.where(m_strict, q[0:L, 0:LANES], 0.0) for q in q4]
    tinv = [jnp.where(eye, 1.0, a) for a in a_ab]
    pw = [_dot(a.astype(BF16), stack(a)) for a in a_ab]
    for _ in range(4):
        both = [_dot(jnp.concatenate([tinv[gp], pw[gp]], axis=0).astype(BF16), stack(pw[gp])) for gp in pairs]
        tinv = [tinv[gp] + both[gp][0:L] for gp in pairs]
        pw = [both[gp][L:] for gp in pairs]
    tinv = [tinv[gp] + _dot(tinv[gp].astype(BF16), stack(pw[gp])) for gp in pairs]
    v_st = [stack(z) for z in vv]
    rhs = [ar[gp][0:L] + _dot(jnp.where(m_strict, q4[gp][0:L, LANES:], 0.0).astype(BF16), v_st[gp])
           for gp in pairs]
    u = [_dot(tinv[gp].astype(BF16), stack(rhs[gp])) for gp in pairs]
    ys = [ar[gp][L:] + _dot(jnp.where(m_incl2, q4[gp][L:], 0.0).astype(BF16),
                            jnp.concatenate([stack(u[gp]), v_st[gp]], axis=0)) for gp in pairs]
    for gp in pairs:
        wl_gp = wl[gp // npair][:, sls[gp][1]]
        uv = jnp.concatenate([u[gp], vv[gp]], axis=0)
        bk = jnp.concatenate([bbar[sls[gp]], kbar[sls[gp]]], axis=0) * wl_gp
        s_add = _dot(uv.T.astype(BF16), bk.astype(BF16))
        state_ref[gp] = s0[gp] * wl_gp + jnp.where(same, s_add, 0.0)
    y = jnp.concatenate([jnp.concatenate(ys[s * npair:(s + 1) * npair], axis=1) for s in range(ns)], axis=0)

    inv_n = 1.0 / HEAD_DIM
    d = y - _dot(y.astype(BF16), ones_bd) * inv_n
    var = _dot((d * d).astype(BF16), ones_bd) * inv_n
    yn = d * lax.rsqrt(var + LNX_EPS) * lnw_ref[...] + lnb_ref[...]
    bonus = _dot((r * kmod * rk_ref[...]).astype(BF16), ones_bd) * v
    y_ref[...] = ((yn + bonus) * g).astype(BF16).reshape(ns, L, RW_DIM)


def _rwkv(p_rw, batch, seq, mu, w0, a0, w_wa, g_up, k_k, k_a, r_k, lnx_w, lnx_b, ones_bd, ns=2):
    nc = seq // CHUNK
    vec = lambda: _const_spec((1, RW_DIM))
    idx = jnp.arange(CHUNK)
    tri = (idx[:, None] >= idx[None, :]).astype(BF16)
    y = pl.pallas_call(
        _rwkv_body,
        grid=(batch // ns, nc),
        in_specs=[
            pl.BlockSpec((ns, CHUNK, RW_COLS), lambda b, c: (b, c, 0)),
            _const_spec((1, RW_COLS)), vec(), vec(),
            _const_spec((LANES, 2 * RW_DIM)), _const_spec((LANES, RW_DIM)),
            vec(), vec(), vec(), vec(), vec(),
            _const_spec((CHUNK, CHUNK)), _const_spec((RW_DIM, RW_DIM)),
        ],
        out_specs=pl.BlockSpec((ns, CHUNK, RW_DIM), lambda b, c: (b, c, 0)),
        out_shape=jax.ShapeDtypeStruct((batch, seq, RW_DIM), BF16),
        scratch_shapes=[
            pltpu.VMEM((8, RW_COLS), F32),
            pltpu.VMEM((ns * RW_DIM // LANES, LANES, LANES), F32),
        ],
        compiler_params=_params("arbitrary", "arbitrary"),
        name="rwkv",
    )(p_rw.reshape(batch, seq, RW_COLS), mu, w0, a0, w_wa, g_up, k_k, k_a, r_k, lnx_w, lnx_b, tri, ones_bd)
    return y.reshape(batch * seq, RW_DIM)


def _bucket_ranges():
    max_exact = NUM_BUCKETS // 2
    ratio_num, ratio_den = REL_MAX_DIST, max_exact
    bounds = []
    for b in range(NUM_BUCKETS):
        if b < max_exact:
            bounds.append((b, b + 1))
            continue
        def first(j):
            n = max_exact
            while (n ** (NUM_BUCKETS - max_exact)) * (ratio_den ** j) < (ratio_num ** j) * (max_exact ** (NUM_BUCKETS - max_exact)):
                n += 1
            return n
        lo = first(b - max_exact)
        hi = first(b + 1 - max_exact) if b < NUM_BUCKETS - 1 else 1 << 30
        bounds.append((lo, hi))
    return bounds


def _t5bias_body(tab_ref, o_ref, bmax_ref, *, ranges):
    h = pl.program_id(0)
    blk = MOBA_BLOCK
    bmax = tab_ref[h, 0]
    for b in range(1, NUM_BUCKETS):
        bmax = jnp.maximum(bmax, tab_ref[h, b])
    bmax_ref[...] = jnp.full(bmax_ref.shape, bmax * LOG2E, F32)
    rel = lax.broadcasted_iota(jnp.int32, (blk, blk), 1) - lax.broadcasted_iota(jnp.int32, (blk, blk), 0)
    nblk = o_ref.shape[1] // blk
    for di in range(nblk):
        dist = rel + di * blk
        n = jnp.maximum(dist, 0)
        acc = jnp.zeros((blk, blk), F32)
        for b, (lo, hi) in enumerate(ranges):
            if hi <= di * blk - (blk - 1) or lo > di * blk + (blk - 1):
                continue
            acc = jnp.where((n >= lo) & (n < hi), tab_ref[h, b] * LOG2E, acc)
        o_ref[0, (nblk - 1 - di) * blk:(nblk - di) * blk, :] = jnp.where(dist >= 0, acc, NEG_INF)


def _t5bias(table, nblk):
    heads = table.shape[0]
    return pl.pallas_call(
        functools.partial(_t5bias_body, ranges=_bucket_ranges()),
        grid=(heads,),
        in_specs=[pl.BlockSpec(memory_space=pltpu.SMEM)],
        out_specs=[pl.BlockSpec((1, nblk * MOBA_BLOCK, MOBA_BLOCK), lambda h: (h, 0, 0)),
                   pl.BlockSpec((1, 8, LANES), lambda h: (h, 0, 0))],
        out_shape=[jax.ShapeDtypeStruct((heads, nblk * MOBA_BLOCK, MOBA_BLOCK), F32),
                   jax.ShapeDtypeStruct((heads, 8, LANES), F32)],
        compiler_params=_params("arbitrary"),
        name="t5bias",
    )(table)


def _moba_body(k_ref, qt_ref, vt_ref, bias_ref, bmax_ref, qg_ref, kg_ref, ones_ref, o_ref, kaug_ref, qaug_ref, vaug_ref,
               ot_ref):
    seq = k_ref.shape[0]
    blk = MOBA_BLOCK
    nb = seq // blk

    k2 = k_ref[...].astype(F32)
    ms = _dot((k2 * k2).astype(BF16), ones_ref[...]) * (1.0 / HEAD_DIM)
    for hh in range(2):
        vaug_ref[hh, 0:HEAD_DIM, :] = vt_ref[hh * HEAD_DIM:(hh + 1) * HEAD_DIM, :]
        vaug_ref[hh, HEAD_DIM:, :] = jnp.ones((vaug_ref.shape[1] - HEAD_DIM, seq), BF16)
    knf = k2 * lax.rsqrt(ms + NORM_EPS) * kg_ref[...]
    kmean = jnp.concatenate(
        [jnp.mean(knf[j * blk:(j + 1) * blk], axis=0, keepdims=True) for j in range(nb)], axis=0)
    klane = lax.broadcasted_iota(jnp.int32, (seq, LANES), 1)
    kblk = lax.broadcasted_iota(jnp.int32, (seq, LANES), 0) // blk
    kaug_ref[0] = jnp.where(klane < HEAD_DIM, knf, jnp.where(klane - HEAD_DIM == kblk, 1.0, 0.0)).astype(BF16)
    kaug_ref[1] = jnp.where(klane >= HEAD_DIM, knf, jnp.where(klane == kblk, 1.0, 0.0)).astype(BF16)

    q2 = qt_ref[...].astype(F32)

    def nrm(qh):
        return qh * lax.rsqrt(jnp.mean(qh * qh, axis=0, keepdims=True) + NORM_EPS)

    qn = jnp.concatenate([nrm(q2[0:HEAD_DIM]), nrm(q2[HEAD_DIM:])], axis=0) * qg_ref[...] \
        * (HEAD_DIM ** -0.5 * LOG2E)
    qnb = qn.astype(BF16)

    km_hi = kmean.astype(BF16)
    km_lo = (kmean - km_hi.astype(F32)).astype(BF16)
    lane = lax.broadcasted_iota(jnp.int32, (nb, LANES), 1)
    rowi = lax.broadcasted_iota(jnp.int32, (nb, seq), 0)
    qblk = lax.broadcasted_iota(jnp.int32, (nb, seq), 1) // blk
    past = rowi < qblk
    qblk1 = qblk[0:1]
    zero_b = jnp.zeros_like(km_hi)
    pad = jnp.zeros((HEAD_DIM - nb, seq), F32)

    qk_max = jnp.max(jnp.abs(qg_ref[...]), axis=0, keepdims=True) \
        * jnp.max(jnp.abs(kg_ref[...]), axis=1, keepdims=True) * (HEAD_DIM ** 0.5 * LOG2E * BOUND_SLACK)
    qk_own = qn * knf.T
    gaps = []

    for hh in range(2):
        hs = slice(hh * HEAD_DIM, (hh + 1) * HEAD_DIM)
        own_bias = bias_ref[hh, (nb - 1) * blk:(nb - 1) * blk + 1, 0:1]
        ub = qk_max + bmax_ref[hh, 0:1, 0:1]
        lb = jnp.sum(qk_own[hs], axis=0, keepdims=True) + own_bias
        gaps.append(ub - lb)
        shift = 0.5 * (ub + lb)

        hm = (lane < HEAD_DIM) if hh == 0 else (lane >= HEAD_DIM)
        gate = _dot(jnp.where(hm, km_hi, zero_b), qnb) + _dot(jnp.where(hm, km_lo, zero_b), qnb)
        rows = []
        for j in range(nb):
            gj = gate[j:j + 1, :]
            beats = (gate > gj) | ((gate == gj) & (rowi < j))
            cnt = jnp.sum(jnp.where(past & beats, 1.0, 0.0), axis=0, keepdims=True)
            sel = ((qblk1 > j) & (cnt < MOBA_TOPK)) | (qblk1 == j)
            rows.append(jnp.where(sel, 0.0, NEG_INF))
        selb = jnp.concatenate(rows, axis=0)
        for variant, sb in ((0, selb), (2, selb - shift)):
            if hh == 0:
                qaug_ref[variant + hh] = jnp.concatenate([qn[0:HEAD_DIM], sb, pad], axis=0).astype(BF16)
            else:
                qaug_ref[variant + hh] = jnp.concatenate([sb, pad, qn[HEAD_DIM:]], axis=0).astype(BF16)

    tiles = [(hh, i) for i in range(nb) for hh in range(2)]

    def scores(variant, hh, i):
        nk = (i + 1) * blk
        return _dot(kaug_ref[hh, 0:nk, :], qaug_ref[variant + hh, :, i * blk:(i + 1) * blk]) \
            + bias_ref[hh, (nb - 1 - i) * blk:, :]

    def attend(variant, probs):
        s_next = scores(variant, *tiles[0])
        for n, (hh, i) in enumerate(tiles):
            s = s_next
            if n + 1 < len(tiles):
                s_next = scores(variant, *tiles[n + 1])
            acc = _dot(vaug_ref[hh, :, 0:(i + 1) * blk], probs(s))
            ot_ref[hh * HEAD_DIM:(hh + 1) * HEAD_DIM, i * blk:(i + 1) * blk] = \
                acc[0:HEAD_DIM] / acc[HEAD_DIM:HEAD_DIM + 1]

    narrow = jnp.max(jnp.maximum(gaps[0], gaps[1])) <= MAX_BRACKET

    @pl.when(narrow)
    def _():
        attend(2, lambda s: jnp.exp2(s).astype(BF16))

    @pl.when(jnp.logical_not(narrow))
    def _():
        attend(0, lambda s: jnp.exp2(s - jnp.max(s, axis=0, keepdims=True)).astype(BF16))

    for i in range(nb):
        o_ref[i * blk:(i + 1) * blk, :] = ot_ref[:, i * blk:(i + 1) * blk].T.astype(BF16)


def _moba(k_att, q_t, v_t, bias_t, bias_max, qg, kg, ones_bd, batch, seq):
    n = batch * seq
    npair = ATT_DIM // LANES
    nb = seq // MOBA_BLOCK
    return pl.pallas_call(
        _moba_body,
        grid=(npair, batch),
        in_specs=[
            pl.BlockSpec((seq, LANES), lambda g, b: (b, g)),
            pl.BlockSpec((LANES, seq), lambda g, b: (g, b)),
            pl.BlockSpec((LANES, seq), lambda g, b: (g, b)),
            pl.BlockSpec((2, seq, MOBA_BLOCK), lambda g, b: (g, 0, 0)),
            pl.BlockSpec((2, 8, LANES), lambda g, b: (g, 0, 0)),
            _const_spec((LANES, 1)), _const_spec((1, LANES)), _const_spec((LANES, LANES)),
        ],
        out_specs=pl.BlockSpec((seq, LANES), lambda g, b: (b, g)),
        out_shape=jax.ShapeDtypeStruct((n, ATT_DIM), BF16),
        scratch_shapes=[
            pltpu.VMEM((2, seq, LANES), BF16),
            pltpu.VMEM((4, LANES, seq), BF16),
            pltpu.VMEM((2, HEAD_DIM + 16, seq), BF16),
            pltpu.VMEM((LANES, seq), F32),
        ],
        compiler_params=_params("arbitrary", "arbitrary"),
        name="moba",
    )(k_att, q_t, v_t, bias_t, bias_max, qg, kg, ones_bd)


def _outproj_body(x_ref, yr_ref, ya_ref, wr_ref, wa_ref, o_ref):
    o_ref[...] = x_ref[...] + _dot(yr_ref[...], wr_ref[...]) + _dot(ya_ref[...], wa_ref[...])


def _outproj(x2, y_rw, y_att, w_r, w_a, tm=2048):
    n = x2.shape[0]
    return pl.pallas_call(
        _outproj_body,
        grid=(n // tm,),
        in_specs=[
            pl.BlockSpec((tm, D_MODEL), lambda i: (i, 0)),
            pl.BlockSpec((tm, RW_DIM), lambda i: (i, 0)),
            pl.BlockSpec((tm, ATT_DIM), lambda i: (i, 0)),
            _const_spec((RW_DIM, D_MODEL)), _const_spec((ATT_DIM, D_MODEL)),
        ],
        out_specs=pl.BlockSpec((tm, D_MODEL), lambda i: (i, 0)),
        out_shape=jax.ShapeDtypeStruct((n, D_MODEL), F32),
        compiler_params=_params("arbitrary"),
        name="outproj",
    )(x2, y_rw, y_att, w_r, w_a)


def _causal_conv3(h, carry_ref, cols, w):
    tm = h.shape[0]
    c = carry_ref[:, cols]
    row = lax.broadcasted_iota(jnp.int32, c.shape, 0)
    r1 = pltpu.roll(h, 1, axis=0)
    r2 = pltpu.roll(h, 2, axis=0)
    h1 = jnp.concatenate([jnp.where(row == 0, c[7:8], r1[0:8]), r1[8:]], axis=0)
    h2 = jnp.concatenate([jnp.where(row == 0, c[6:7], jnp.where(row == 1, c[7:8], r2[0:8])), r2[8:]], axis=0)
    carry_ref[:, cols] = h[tm - 8:tm]
    return w[2:3] * h + w[1:2] * h1 + w[0:1] * h2


def _ffn_body(x_ref, g_ref, up_ref, cw_ref, cb_ref, down_ref, o_ref, xn_ref, z_ref, carry_ref, *, tc):
    @pl.when(pl.program_id(1) == 0)
    def _():
        carry_ref[...] = jnp.zeros_like(carry_ref)

    xn_ref[...] = _rms_norm_rows(x_ref[...], g_ref[...]).astype(BF16)

    def branch(c0):
        cols = slice(c0, c0 + tc)
        h = _dot(xn_ref[...], up_ref[:, cols])
        return _causal_conv3(h, carry_ref, cols, cw_ref[:, cols]) + cb_ref[:, cols]

    for c0 in range(0, D_FF, tc):
        za = branch(c0)
        zu = branch(D_FF + c0)
        z_ref[:, c0:c0 + tc] = (za * jax.nn.sigmoid(za) * zu).astype(BF16)
    o_ref[...] = x_ref[...] + _dot(z_ref[...], down_ref[...])


def _ffn(x2, batch, seq, g, up, conv_w, conv_b, down, tm=1024, tc=256):
    n = batch * seq
    nt = seq // tm
    return pl.pallas_call(
        functools.partial(_ffn_body, tc=tc),
        grid=(batch, nt),
        in_specs=[
            pl.BlockSpec((tm, D_MODEL), lambda b, t: (b * nt + t, 0)),
            _const_spec((1, D_MODEL)),
            _const_spec((D_MODEL, 2 * D_FF)),
            _const_spec((3, 2 * D_FF)),
            _const_spec((1, 2 * D_FF)),
            _const_spec((D_FF, D_MODEL)),
        ],
        out_specs=pl.BlockSpec((tm, D_MODEL), lambda b, t: (b * nt + t, 0)),
        out_shape=jax.ShapeDtypeStruct((n, D_MODEL), F32),
        scratch_shapes=[pltpu.VMEM((tm, D_MODEL), BF16), pltpu.VMEM((tm, D_FF), BF16),
                        pltpu.VMEM((8, 2 * D_FF), F32)],
        compiler_params=_params("arbitrary", "arbitrary"),
        name="ffn",
    )(x2, g, up, conv_w, conv_b, down)


def _sconv_body(x_ref, g_ref, win_ref, cw_ref, wout_ref, o_ref, xn_ref, z_ref, carry_ref, *, tc):
    @pl.when(pl.program_id(1) == 0)
    def _():
        carry_ref[...] = jnp.zeros_like(carry_ref)

    xn_ref[...] = _rms_norm_rows(x_ref[...], g_ref[...]).astype(BF16)

    for c0 in range(0, D_MODEL, tc):
        cols = slice(c0, c0 + tc)
        xn = xn_ref[...]
        b_gate = _dot(xn, win_ref[:, cols])
        c_gate = _dot(xn, win_ref[:, D_MODEL + c0:D_MODEL + c0 + tc])
        hid = _dot(xn, win_ref[:, 2 * D_MODEL + c0:2 * D_MODEL + c0 + tc])
        conv = _causal_conv3(c_gate * hid, carry_ref, cols, cw_ref[:, cols])
        z_ref[:, cols] = (b_gate * conv).astype(BF16)
    o_ref[...] = x_ref[...] + _dot(z_ref[...], wout_ref[...])


def _sconv(x2, batch, seq, g, w_in, conv_w, w_out, tm=1024, tc=256):
    n = batch * seq
    nt = seq // tm
    return pl.pallas_call(
        functools.partial(_sconv_body, tc=tc),
        grid=(batch, nt),
        in_specs=[
            pl.BlockSpec((tm, D_MODEL), lambda b, t: (b * nt + t, 0)),
            _const_spec((1, D_MODEL)),
            _const_spec((D_MODEL, 3 * D_MODEL)),
            _const_spec((3, D_MODEL)),
            _const_spec((D_MODEL, D_MODEL)),
        ],
        out_specs=pl.BlockSpec((tm, D_MODEL), lambda b, t: (b * nt + t, 0)),
        out_shape=jax.ShapeDtypeStruct((n, D_MODEL), F32),
        scratch_shapes=[pltpu.VMEM((tm, D_MODEL), BF16), pltpu.VMEM((tm, D_MODEL), BF16),
                        pltpu.VMEM((8, D_MODEL), F32)],
        compiler_params=_params("arbitrary", "arbitrary"),
        name="sconv",
    )(x2, g, w_in, conv_w, w_out)


def _block_diag_ones(n):
    i = jnp.arange(n) // HEAD_DIM
    return (i[:, None] == i[None, :]).astype(BF16)


def _rwkv_moba_mixer(x2, batch, seq, rel_bias, mix_norm, w_in, shift_mu, w0, w_lora_up, a0, a_lora_up, g_lora_up,
                     k_k, k_a, r_k, lnx_w, lnx_b, q_norm, k_norm, w_out):
    row = lambda z: z.reshape(1, -1).astype(F32)
    w_in = w_in.astype(BF16)
    q0 = RW_COLS
    w_rw = w_in[:, :q0]
    w_qt = w_in[:, q0:q0 + ATT_DIM].T
    w_k = w_in[:, q0 + ATT_DIM:q0 + 2 * ATT_DIM]
    w_vt = w_in[:, q0 + 2 * ATT_DIM:].T
    p_rw, k_att, q_t, v_t = _inproj0(x2, row(mix_norm), w_rw, w_k, w_qt, w_vt)

    zeros = jnp.zeros_like(w_lora_up)
    w_wa = jnp.concatenate([jnp.concatenate([w_lora_up, zeros], axis=1),
                            jnp.concatenate([zeros, a_lora_up], axis=1)], axis=0).astype(BF16)
    y_rw = _rwkv(p_rw, batch, seq, row(shift_mu), row(w0), row(a0), w_wa, g_lora_up.astype(BF16),
                 row(k_k), row(k_a), row(r_k), row(lnx_w), row(lnx_b), _block_diag_ones(RW_DIM),
                 ns=8 if batch % 8 == 0 else 1)

    bias_t, bias_max = _t5bias(rel_bias.T.astype(F32), seq // MOBA_BLOCK)
    qg = jnp.tile(q_norm.astype(F32), 2).reshape(LANES, 1)
    kg = jnp.tile(k_norm.astype(F32), 2).reshape(1, LANES)
    y_att = _moba(k_att, q_t, v_t, bias_t, bias_max, qg, kg, _block_diag_ones(LANES), batch, seq)

    w_out = w_out.astype(BF16)
    return _outproj(x2, y_rw, y_att, w_out[:RW_DIM], w_out[RW_DIM:])


def kernel(x, rel_bias, l0_mix_norm, l0_w_in, l0_shift_mu, l0_w0, l0_w_lora_up, l0_a0, l0_a_lora_up, l0_g_lora_up, l0_k_k, l0_k_a, l0_r_k, l0_lnx_w, l0_lnx_b, l0_q_norm, l0_k_norm, l0_w_out, l0_ffn_norm, l0_ffn_up, l0_ffn_conv_w, l0_ffn_conv_b, l0_ffn_down, l1_mix_norm, l1_w_in, l1_conv_w, l1_w_out, l1_ffn_norm, l1_ffn_up, l1_ffn_conv_w, l1_ffn_conv_b, l1_ffn_down):
    batch, seq, d = x.shape
    row = lambda z: z.reshape(1, -1).astype(F32)
    x2 = x.reshape(batch * seq, d)
    x2 = _rwkv_moba_mixer(x2, batch, seq, rel_bias, l0_mix_norm, l0_w_in, l0_shift_mu, l0_w0, l0_w_lora_up, l0_a0,
                          l0_a_lora_up, l0_g_lora_up, l0_k_k, l0_k_a, l0_r_k, l0_lnx_w, l0_lnx_b, l0_q_norm,
                          l0_k_norm, l0_w_out)
    x2 = _ffn(x2, batch, seq, row(l0_ffn_norm), l0_ffn_up.astype(BF16), l0_ffn_conv_w.astype(F32),
              row(l0_ffn_conv_b), l0_ffn_down.astype(BF16))
    x2 = _sconv(x2, batch, seq, row(l1_mix_norm), l1_w_in.astype(BF16), l1_conv_w.astype(F32),
                l1_w_out.astype(BF16))
    x2 = _ffn(x2, batch, seq, row(l1_ffn_norm), l1_ffn_up.astype(BF16), l1_ffn_conv_w.astype(F32),
              row(l1_ffn_conv_b), l1_ffn_down.astype(BF16))
    return x2.reshape(batch, seq, d)
```
